```python
import math
import jax, jax.numpy as jnp
from jax import lax
import numpy as np

D_MODEL = 4096
BATCH = 4
SEQ = 4096
DEPTH = 2

HEAD_DIM = 128
DIFF_HEADS = 8
DIFF_QK = 2 * HEAD_DIM
DIFF_V = 2 * HEAD_DIM
DIFF_QBLOCK = 128
ML_HEADS = 8
ML_QK = HEAD_DIM
ML_V = 2 * HEAD_DIM
ML_CHUNK = 64
CONV_W = 4
MOBA_HEADS = 16
MOBA_BLOCK = 256
MOBA_TOPK = 3
MOBA_QCHUNK = 16
N_BRANCH = 3
BRANCH_W = DIFF_HEADS * DIFF_V
D_FF = -(-8 * D_MODEL // (3 * 256)) * 256
EPS = 1e-6

IN_SIZES = (
    DIFF_HEADS * DIFF_QK, DIFF_HEADS * DIFF_QK, DIFF_HEADS * DIFF_V,
    ML_HEADS * ML_QK, ML_HEADS * ML_QK, ML_HEADS * ML_V, ML_HEADS * ML_V,
    ML_HEADS, ML_HEADS,
    MOBA_HEADS * HEAD_DIM, MOBA_HEADS * HEAD_DIM, MOBA_HEADS * HEAD_DIM,
    N_BRANCH * D_MODEL,
)
D_IN = int(sum(IN_SIZES))
SPLIT_AT = tuple(int(s) for s in np.cumsum(IN_SIZES)[:-1])

kernel_name = "hybrid_diffattn_mlstm_moba_swiglu"


def rms_norm(x, g):
    xf = x.astype(jnp.float32)
    y = xf * lax.rsqrt(jnp.mean(xf * xf, axis=-1, keepdims=True) + EPS)
    return (y * g.astype(jnp.float32)).astype(x.dtype)


def head_layer_norm(x, g):
    H, Dv = x.shape[-2], x.shape[-1]
    xf = x.astype(jnp.float32)
    mu = jnp.mean(xf, axis=-1, keepdims=True)
    xc = xf - mu
    y = xc * lax.rsqrt(jnp.mean(xc * xc, axis=-1, keepdims=True) + EPS)
    return y * g.astype(jnp.float32).reshape(H, Dv)


def alibi_slopes(n):
    return jnp.asarray(2.0 ** (-8.0 * np.arange(1, n + 1) / n), dtype=jnp.float32)


def causal_conv(x, w, b):
    C = x.shape[-1]
    y = lax.conv_general_dilated(
        x, w[:, None, :].astype(x.dtype), window_strides=(1,),
        padding=((w.shape[0] - 1, 0),), dimension_numbers=("NWC", "WIO", "NWC"),
        feature_group_count=C)
    return y + b.astype(x.dtype)


def diff_attention(q, k, v, lam, slopes):
    B, H, _, S, Dh = q.shape
    Dv = v.shape[-1]
    scale = Dh ** -0.5
    pos_k = jnp.arange(S)

    def block(i):
        t0 = i * DIFF_QBLOCK
        qb = lax.dynamic_slice_in_dim(q, t0, DIFF_QBLOCK, axis=3)
        pos_q = t0 + jnp.arange(DIFF_QBLOCK)
        s = jnp.einsum("bhcqd,bhckd->bhcqk", qb, k, preferred_element_type=jnp.float32) * scale
        dist = (pos_q[:, None] - pos_k[None, :]).astype(jnp.float32)
        s = jnp.where(dist >= 0, s - slopes[:, None, None, None] * dist, -jnp.inf)
        p = jax.nn.softmax(s, axis=-1)
        a = p[:, :, 0] - lam * p[:, :, 1]
        return jnp.einsum("bhqk,bhkd->bhqd", a.astype(v.dtype), v)

    out = lax.map(block, jnp.arange(S // DIFF_QBLOCK))
    return out.transpose(1, 0, 3, 2, 4).reshape(B, S, H, Dv)


def mlstm(q, k, v, i_pre, f_pre):
    B, S, H, Dk = q.shape
    Dv = v.shape[-1]
    L = ML_CHUNK
    NC = S // L
    f32 = jnp.float32

    def to_chunks(a):
        a = a.reshape((B, NC, L, H) + a.shape[3:])
        return jnp.moveaxis(a, (1, 3), (0, 2))

    qc = to_chunks(q.astype(f32))
    kc = to_chunks(k.astype(f32) * (Dk ** -0.5))
    vc = to_chunks(v.astype(f32))
    ic = to_chunks(i_pre.astype(f32))
    lfc = to_chunks(jax.nn.log_sigmoid(f_pre.astype(f32)))
    tri = jnp.tril(jnp.ones((L, L), dtype=bool))

    def step(carry, xs):
        C, n, m = carry
        qq, kk, vv, ii, lf = xs
        g = jnp.cumsum(lf, axis=-1)
        gL = g[..., -1]
        D = jnp.where(tri, g[..., :, None] - g[..., None, :] + ii[..., None, :], -jnp.inf)
        inter = g + m[..., None]
        m_t = jnp.maximum(inter, jnp.max(D, axis=-1))
        w = jnp.exp(D - m_t[..., None])
        a = jnp.exp(inter - m_t)
        sw = jnp.einsum("bhtd,bhsd->bhts", qq, kk) * w
        num = a[..., None] * jnp.einsum("bhvd,bhtd->bhtv", C, qq) + jnp.einsum("bhts,bhsv->bhtv", sw, vv)
        den = a * jnp.einsum("bhd,bhtd->bht", n, qq) + jnp.sum(sw, axis=-1)
        h = num / jnp.maximum(jnp.abs(den), jnp.exp(-m_t))[..., None]
        aL = gL[..., None] - g + ii
        m_new = jnp.maximum(gL + m, jnp.max(aL, axis=-1))
        wL = jnp.exp(aL - m_new[..., None])
        decay = jnp.exp(gL + m - m_new)
        C_new = decay[..., None, None] * C + jnp.einsum("bhs,bhsv,bhsd->bhvd", wL, vv, kk)
        n_new = decay[..., None] * n + jnp.einsum("bhs,bhsd->bhd", wL, kk)
        return (C_new, n_new, m_new), h

    init = (jnp.zeros((B, H, Dv, Dk), f32), jnp.zeros((B, H, Dk), f32), jnp.zeros((B, H), f32))
    _, hs = lax.scan(step, init, (qc, kc, vc, ic, lfc))
    return hs.transpose(1, 0, 3, 2, 4).reshape(B, S, H, Dv)


def moba_attention(q, k, v, slopes):
    B, S, H, Dh = q.shape
    BLK = MOBA_BLOCK
    QC = MOBA_QCHUNK
    NB = -(-S // BLK)
    Sp = NB * BLK
    f32 = jnp.float32
    scale = Dh ** -0.5
    q = q.transpose(0, 2, 1, 3)
    pad = ((0, 0), (0, 0), (0, Sp - S), (0, 0))
    kp = jnp.pad(k.transpose(0, 2, 1, 3), pad)
    vp = jnp.pad(v.transpose(0, 2, 1, 3), pad)
    kb = kp.reshape(B, H, NB, BLK, Dh)
    vb = vp.reshape(B, H, NB, BLK, Dh)
    kmean = jnp.mean(kb.astype(f32), axis=3)
    n_sel = min(MOBA_TOPK, NB)
    bi = jnp.arange(B)[:, None, None, None]
    hi = jnp.arange(H)[None, :, None, None]
    blk_ids = jnp.arange(NB)
    off = jnp.arange(BLK)

    def chunk(c):
        t0 = c * QC
        qc = lax.dynamic_slice_in_dim(q, t0, QC, axis=2)
        pos_q = t0 + jnp.arange(QC)
        own = t0 // BLK
        gs = jnp.einsum("bhqd,bhnd->bhqn", qc.astype(f32), kmean)
        gs = jnp.where(blk_ids < own, gs, -jnp.inf)
        _, idx = lax.top_k(gs, n_sel)
        valid = idx < own
        ks = kb[bi, hi, idx]
        vs = vb[bi, hi, idx]
        s_sel = jnp.einsum("bhqd,bhqnkd->bhqnk", qc, ks, preferred_element_type=f32) * scale
        dist_sel = (pos_q[:, None, None] - (idx[..., None] * BLK + off)).astype(f32)
        s_sel = jnp.where(valid[..., None], s_sel - slopes[:, None, None, None] * dist_sel, -jnp.inf)
        k_own = lax.dynamic_slice_in_dim(kp, own * BLK, BLK, axis=2)
        v_own = lax.dynamic_slice_in_dim(vp, own * BLK, BLK, axis=2)
        s_own = jnp.einsum("bhqd,bhkd->bhqk", qc, k_own, preferred_element_type=f32) * scale
        dist_own = (pos_q[:, None] - (own * BLK + off)[None, :]).astype(f32)
        s_own = jnp.where(dist_own >= 0, s_own - slopes[:, None, None] * dist_own, -jnp.inf)
        s = jnp.concatenate([s_sel.reshape(B, H, QC, n_sel * BLK), s_own], axis=-1)
        p = jax.nn.softmax(s, axis=-1).astype(v.dtype)
        p_sel = p[..., : n_sel * BLK].reshape(B, H, QC, n_sel, BLK)
        p_own = p[..., n_sel * BLK:]
        return (jnp.einsum("bhqnk,bhqnkd->bhqd", p_sel, vs)
                + jnp.einsum("bhqk,bhkd->bhqd", p_own, v_own))

    out = lax.map(chunk, jnp.arange(S // QC))
    return out.transpose(1, 0, 3, 2, 4).reshape(B, S, H * Dh)


def mixer_sublayer(x, layer_idx, norm1_g, w_in, diff_lambda, diff_norm_g, ml_conv_w, ml_conv_b,
                   ml_gate_b, ml_norm_g, w_branch, w_out):
    B, S, _ = x.shape
    h = rms_norm(x, norm1_g)
    z = h @ w_in
    (dq, dk, dv, mq, mk, mv, mo, mi, mf, bq, bk, bv, gates) = jnp.split(z, SPLIT_AT, axis=-1)

    lam_init = 0.8 - 0.6 * math.exp(-0.3 * layer_idx)
    lam_p = diff_lambda.astype(jnp.float32)
    lam = jnp.exp(jnp.sum(lam_p[0] * lam_p[1])) - jnp.exp(jnp.sum(lam_p[2] * lam_p[3])) + lam_init
    qa = dq.reshape(B, S, DIFF_HEADS, 2, HEAD_DIM).transpose(0, 2, 3, 1, 4)
    ka = dk.reshape(B, S, DIFF_HEADS, 2, HEAD_DIM).transpose(0, 2, 3, 1, 4)
    va = dv.reshape(B, S, DIFF_HEADS, DIFF_V).transpose(0, 2, 1, 3)
    ya = diff_attention(qa, ka, va, lam, alibi_slopes(DIFF_HEADS))
    ya = (rms_norm(ya, diff_norm_g) * (1.0 - lam_init)).reshape(B, S, BRANCH_W)

    qk = jax.nn.silu(causal_conv(jnp.concatenate([mq, mk], axis=-1), ml_conv_w, ml_conv_b))
    q_m, k_m = jnp.split(qk, 2, axis=-1)
    hm = mlstm(q_m.reshape(B, S, ML_HEADS, ML_QK), k_m.reshape(B, S, ML_HEADS, ML_QK),
               mv.reshape(B, S, ML_HEADS, ML_V), mi + ml_gate_b[0], mf + ml_gate_b[1])
    yb = head_layer_norm(hm, ml_norm_g).reshape(B, S, BRANCH_W) * jax.nn.sigmoid(mo.astype(jnp.float32))
    yb = yb.astype(x.dtype)

    yc = moba_attention(bq.reshape(B, S, MOBA_HEADS, HEAD_DIM), bk.reshape(B, S, MOBA_HEADS, HEAD_DIM),
                        bv.reshape(B, S, MOBA_HEADS, HEAD_DIM), alibi_slopes(MOBA_HEADS))

    g = jax.nn.sigmoid(gates.astype(jnp.float32)).reshape(B, S, N_BRANCH, D_MODEL).astype(x.dtype)
    merged = g[:, :, 0] * (ya @ w_branch[0])
    merged = merged + g[:, :, 1] * (yb @ w_branch[1])
    merged = merged + g[:, :, 2] * (yc @ w_branch[2])
    return merged @ w_out


def swiglu(x, w_gate_up, w_down):
    gu = x @ w_gate_up
    gate, up = jnp.split(gu, 2, axis=-1)
    return (jax.nn.silu(gate) * up) @ w_down


def setup_inputs(seed: int = 0) -> dict:
    key = jax.random.key(seed)
    ks = jax.random.split(key, 16)
    f32 = jnp.float32
    nrm = lambda k, shape, s: jax.random.normal(k, shape, f32) * s
    gate_b = jnp.stack([
        nrm(ks[6], (DEPTH, ML_HEADS), 0.1),
        3.0 + 3.0 * jax.random.uniform(ks[7], (DEPTH, ML_HEADS), f32),
    ], axis=1)
    return {
        "x": nrm(ks[0], (BATCH, SEQ, D_MODEL), 1.0),
        "norm1_g": 1.0 + nrm(ks[1], (DEPTH, D_MODEL), 0.02),
        "w_in": nrm(ks[2], (DEPTH, D_MODEL, D_IN), D_MODEL ** -0.5),
        "diff_lambda": nrm(ks[3], (DEPTH, 4, HEAD_DIM), 0.1),
        "diff_norm_g": 1.0 + nrm(ks[4], (DEPTH, DIFF_V), 0.02),
        "ml_conv_w": nrm(ks[5], (DEPTH, CONV_W, 2 * ML_HEADS * ML_QK), CONV_W ** -0.5),
        "ml_conv_b": nrm(ks[8], (DEPTH, 2 * ML_HEADS * ML_QK), 0.01),
        "ml_gate_b": gate_b,
        "ml_norm_g": 1.0 + nrm(ks[9], (DEPTH, ML_HEADS * ML_V), 0.02),
        "w_branch": nrm(ks[10], (DEPTH, N_BRANCH, BRANCH_W, D_MODEL), BRANCH_W ** -0.5),
        "w_out": nrm(ks[11], (DEPTH, D_MODEL, D_MODEL), D_MODEL ** -0.5),
        "norm2_g": 1.0 + nrm(ks[12], (DEPTH, D_MODEL), 0.02),
        "w_gate_up": nrm(ks[13], (DEPTH, D_MODEL, 2 * D_FF), D_MODEL ** -0.5),
        "w_down": nrm(ks[14], (DEPTH, D_FF, D_MODEL), D_FF ** -0.5),
        "final_g": 1.0 + nrm(ks[15], (D_MODEL,), 0.02),
    }


def reference(x, norm1_g, w_in, diff_lambda, diff_norm_g, ml_conv_w, ml_conv_b, ml_gate_b, ml_norm_g,
              w_branch, w_out, norm2_g, w_gate_up, w_down, final_g):
    for l in range(DEPTH):
        x = x + mixer_sublayer(x, l, norm1_g[l], w_in[l], diff_lambda[l], diff_norm_g[l], ml_conv_w[l],
                               ml_conv_b[l], ml_gate_b[l], ml_norm_g[l], w_branch[l], w_out[l]).astype(x.dtype)
        x = x + swiglu(rms_norm(x, norm2_g[l]), w_gate_up[l], w_down[l]).astype(x.dtype)
    return rms_norm(x, final_g)
```

```python
import functools
import math

import jax
import jax.numpy as jnp
from jax import lax
from jax.experimental import pallas as pl
from jax.experimental.pallas import tpu as pltpu

F32 = jnp.float32
BF16 = jnp.bfloat16

HEAD_DIM = 128
DIFF_HEADS = 8
DIFF_V = 2 * HEAD_DIM
ML_HEADS = 8
ML_V = 2 * HEAD_DIM
CONV_W = 4
MOBA_HEADS = 16
MOBA_BLOCK = 256
MOBA_TOPK = 3
N_BRANCH = 3
EPS = 1e-6
NEG = -1e30

LANES = 128
SUBLANES = 8
VMEM_LIMIT = 48 * 1024 * 1024

NT_DIMS = (((1,), (1,)), ((), ()))
TN_DIMS = (((0,), (0,)), ((), ()))


def _params(*sem):
    return pltpu.CompilerParams(dimension_semantics=sem, vmem_limit_bytes=VMEM_LIMIT)


def _rmsnorm_body(x_ref, g_ref, o_ref):
    x = x_ref[...]
    ms = jnp.mean(x * x, axis=-1, keepdims=True)
    o_ref[...] = (x * lax.rsqrt(ms + EPS) * g_ref[...]).astype(o_ref.dtype)


def _rmsnorm(x, g, out_dtype, tr=256):
    T, D = x.shape
    tr = min(tr, T)
    return pl.pallas_call(
        _rmsnorm_body,
        grid=(T // tr,),
        in_specs=[pl.BlockSpec((tr, D), lambda i: (i, 0)), pl.BlockSpec((1, D), lambda i: (0, 0))],
        out_specs=pl.BlockSpec((tr, D), lambda i: (i, 0)),
        out_shape=jax.ShapeDtypeStruct((T, D), out_dtype),
        compiler_params=_params("parallel"),
        name="rmsnorm",
    )(x, g.reshape(1, D).astype(F32))


def _mm_body(a_ref, b_ref, o_ref):
    o_ref[...] = jnp.dot(a_ref[...], b_ref[...], preferred_element_type=F32).astype(o_ref.dtype)


def _mm_res_body(a_ref, b_ref, r_ref, o_ref):
    o_ref[...] = (r_ref[...] + jnp.dot(a_ref[...], b_ref[...], preferred_element_type=F32)).astype(o_ref.dtype)


def _matmul(a, b, out_dtype, tm, tn, residual=None, name="matmul"):
    M, K = a.shape
    N = b.shape[1]
    tm, tn = min(tm, M), min(tn, N)
    assert M % tm == 0 and N % tn == 0
    in_specs = [pl.BlockSpec((tm, K), lambda i, j: (i, 0)), pl.BlockSpec((K, tn), lambda i, j: (0, j))]
    args = [a, b]
    body = _mm_body
    if residual is not None:
        in_specs.append(pl.BlockSpec((tm, tn), lambda i, j: (i, j)))
        args.append(residual)
        body = _mm_res_body
    return pl.pallas_call(
        body,
        grid=(M // tm, N // tn),
        in_specs=in_specs,
        out_specs=pl.BlockSpec((tm, tn), lambda i, j: (i, j)),
        out_shape=jax.ShapeDtypeStruct((M, N), out_dtype),
        compiler_params=_params("parallel", "parallel"),
        name=name,
    )(*args)


def _glu_body(a_ref, wg_ref, wu_ref, o_ref):
    a = a_ref[...]
    g = jnp.dot(a, wg_ref[...], preferred_element_type=F32)
    u = jnp.dot(a, wu_ref[...], preferred_element_type=F32)
    o_ref[...] = (g * jax.nn.sigmoid(g) * u).astype(o_ref.dtype)


def _swiglu_up(a, w_gu, tm=1024, tn=256):
    M, K = a.shape
    F = w_gu.shape[1] // 2
    tm = min(tm, M)
    nj = F // tn
    assert F % tn == 0 and M % tm == 0
    return pl.pallas_call(
        _glu_body,
        grid=(M // tm, nj),
        in_specs=[pl.BlockSpec((tm, K), lambda i, j: (i, 0)),
                  pl.BlockSpec((K, tn), lambda i, j: (0, j)),
                  pl.BlockSpec((K, tn), lambda i, j: (0, nj + j))],
        out_specs=pl.BlockSpec((tm, tn), lambda i, j: (i, j)),
        out_shape=jax.ShapeDtypeStruct((M, F), BF16),
        compiler_params=_params("parallel", "parallel"),
        name="swiglu_up",
    )(a, w_gu, w_gu)


def _merge_body(ya_ref, yb_ref, yc_ref, w0_ref, w1_ref, w2_ref, g0_ref, g1_ref, g2_ref, o_ref):
    acc = jax.nn.sigmoid(g0_ref[...].astype(F32)) * jnp.dot(ya_ref[...], w0_ref[...], preferred_element_type=F32)
    acc += jax.nn.sigmoid(g1_ref[...].astype(F32)) * jnp.dot(yb_ref[...], w1_ref[...], preferred_element_type=F32)
    acc += jax.nn.sigmoid(g2_ref[...].astype(F32)) * jnp.dot(yc_ref[...], w2_ref[...], preferred_element_type=F32)
    o_ref[...] = acc.astype(o_ref.dtype)


def _merge(ya, yb, yc, w_branch, gates, tm=512, tn=512):
    M, K = ya.shape
    D = w_branch.shape[2]
    tm, tn = min(tm, M), min(tn, D)
    nj = D // tn
    y_spec = pl.BlockSpec((tm, K), lambda i, j: (i, 0))
    w_specs = [pl.BlockSpec((None, K, tn), functools.partial(lambda i, j, b: (b, 0, j), b=b)) for b in range(N_BRANCH)]
    g_specs = [pl.BlockSpec((tm, tn), functools.partial(lambda i, j, b: (i, b * nj + j), b=b)) for b in range(N_BRANCH)]
    return pl.pallas_call(
        _merge_body,
        grid=(M // tm, nj),
        in_specs=[y_spec, y_spec, y_spec] + w_specs + g_specs,
        out_specs=pl.BlockSpec((tm, tn), lambda i, j: (i, j)),
        out_shape=jax.ShapeDtypeStruct((M, D), BF16),
        compiler_params=_params("parallel", "parallel"),
        name="merge",
    )(ya, yb, yc, w_branch, w_branch, w_branch, gates, gates, gates)


def _softmax_step(t, shift, v, m_ref, l_ref, acc_ref, idx):
    m_prev = m_ref[idx]
    m_new = jnp.maximum(m_prev, jnp.max(t, axis=-1, keepdims=True) + shift)
    p = jnp.exp(t - (m_new - shift))
    alpha = jnp.exp(m_prev - m_new)
    l_ref[idx] = alpha * l_ref[idx] + jnp.sum(p, axis=-1, keepdims=True)
    acc_ref[idx] = alpha * acc_ref[idx] + jnp.dot(p.astype(BF16), v, preferred_element_type=F32)
    m_ref[idx] = m_new


def _diff_body(slopes_ref, lam_ref, q_ref, k_ref, v_ref, g_ref, o_ref, m_ref, l_ref, acc_ref, *, tq, lam_init):
    h = pl.program_id(1)
    qi = pl.program_id(2)
    slope = slopes_ref[h]
    scale = HEAD_DIM ** -0.5
    lp = lam_ref[...]
    lam = (jnp.exp(jnp.sum(lp[0:1] * lp[1:2], axis=-1, keepdims=True))
           - jnp.exp(jnp.sum(lp[2:3] * lp[3:4], axis=-1, keepdims=True)) + lam_init)
    rows = lax.broadcasted_iota(jnp.int32, (tq, tq), 0)
    cols = lax.broadcasted_iota(jnp.int32, (tq, tq), 1)
    rel = (cols - rows).astype(F32) * slope
    m_ref[...] = jnp.full(m_ref.shape, NEG, F32)
    l_ref[...] = jnp.zeros(l_ref.shape, F32)
    acc_ref[...] = jnp.zeros(acc_ref.shape, F32)

    def step(j, masked):
        k0 = pl.multiple_of(j * tq, tq)
        shift = -slope * ((qi - j) * tq).astype(F32)
        v = v_ref[pl.ds(k0, tq), :]
        for c in range(2):
            q = q_ref[:, c * HEAD_DIM:(c + 1) * HEAD_DIM]
            k = k_ref[pl.ds(k0, tq), c * HEAD_DIM:(c + 1) * HEAD_DIM]
            t = lax.dot_general(q, k, NT_DIMS, preferred_element_type=F32) * scale + rel
            if masked:
                t = jnp.where(rows >= cols, t, NEG)
            _softmax_step(t, shift, v, m_ref, l_ref, acc_ref, c)

    def loop_body(j, carry):
        step(j, False)
        return carry

    lax.fori_loop(0, qi, loop_body, 0)
    step(qi, True)

    o = acc_ref[0] / l_ref[0] - lam * (acc_ref[1] / l_ref[1])
    ms = jnp.mean(o * o, axis=-1, keepdims=True)
    y = (o * lax.rsqrt(ms + EPS) * g_ref[...]) * (1.0 - lam_init)
    o_ref[...] = y.astype(o_ref.dtype)


def _diff_attention(zd, diff_lambda, norm_g, slopes, batch, lam_init, tq=256):
    T = zd.shape[0]
    S = T // batch
    H = DIFF_HEADS
    nq = S // tq
    W = 2 * HEAD_DIM
    smem = pl.BlockSpec(memory_space=pltpu.SMEM)
    return pl.pallas_call(
        functools.partial(_diff_body, tq=tq, lam_init=lam_init),
        grid=(batch, H, nq),
        in_specs=[smem,
                  pl.BlockSpec((4, HEAD_DIM), lambda b, h, i: (0, 0)),
                  pl.BlockSpec((tq, W), lambda b, h, i: (b * nq + i, h)),
                  pl.BlockSpec((S, W), lambda b, h, i: (b, H + h)),
                  pl.BlockSpec((S, W), lambda b, h, i: (b, 2 * H + h)),
                  pl.BlockSpec((1, W), lambda b, h, i: (0, 0))],
        out_specs=pl.BlockSpec((tq, W), lambda b, h, i: (b * nq + i, h)),
        out_shape=jax.ShapeDtypeStruct((T, H * W), BF16),
        scratch_shapes=[pltpu.VMEM((2, tq, 1), F32), pltpu.VMEM((2, tq, 1), F32), pltpu.VMEM((2, tq, W), F32)],
        compiler_params=_params("parallel", "parallel", "parallel"),
        name="diff_attention",
    )(slopes, diff_lambda.astype(F32), zd, zd, zd, norm_g.reshape(1, W).astype(F32))


def _moba_body(slopes_ref, q_ref, k_ref, v_ref, o_ref, km_ref, m_ref, l_ref, acc_ref, *, nb):
    h = pl.program_id(1)
    own = pl.program_id(2)
    BLK = MOBA_BLOCK
    slope = slopes_ref[h]
    scale = HEAD_DIM ** -0.5

    @pl.when(own == 0)
    def _():
        km_ref[...] = jnp.zeros(km_ref.shape, F32)
        for n in range(nb):
            kb = k_ref[n * BLK:(n + 1) * BLK, :].astype(F32)
            km_ref[n:n + 1, :] = jnp.mean(kb, axis=0, keepdims=True)

    q = q_ref[...]
    km = km_ref[...]
    km_hi = km.astype(BF16)
    km_lo = (km - km_hi.astype(F32)).astype(BF16)
    gs = (lax.dot_general(q, km_hi, NT_DIMS, preferred_element_type=F32)
          + lax.dot_general(q, km_lo, NT_DIMS, preferred_element_type=F32))
    lane = lax.broadcasted_iota(jnp.int32, (BLK, LANES), 1)
    rank = jnp.zeros((BLK, LANES), F32)
    for m in range(nb):
        col = gs[:, m:m + 1]
        tie = jnp.where(lane > m, 1.0, 0.0)
        beats = jnp.where(col > gs, 1.0, jnp.where(col == gs, tie, 0.0))
        rank = rank + beats * jnp.where(m < own, 1.0, 0.0)
    chosen = jnp.where(lane < own, jnp.where(rank < float(min(MOBA_TOPK, nb)), 1.0, 0.0), 0.0)

    rows = lax.broadcasted_iota(jnp.int32, (BLK, BLK), 0)
    cols = lax.broadcasted_iota(jnp.int32, (BLK, BLK), 1)
    rel = (cols - rows).astype(F32) * slope
    m_ref[...] = jnp.full(m_ref.shape, NEG, F32)
    l_ref[...] = jnp.zeros(l_ref.shape, F32)
    acc_ref[...] = jnp.zeros(acc_ref.shape, F32)

    k0 = pl.multiple_of(own * BLK, BLK)
    t = lax.dot_general(q, k_ref[pl.ds(k0, BLK), :], NT_DIMS, preferred_element_type=F32) * scale + rel
    t = jnp.where(rows >= cols, t, NEG)
    _softmax_step(t, 0.0, v_ref[pl.ds(k0, BLK), :], m_ref, l_ref, acc_ref, 0)

    def past_block(n, carry):
        picked = jnp.sum(jnp.where(lane == n, chosen, 0.0), axis=-1, keepdims=True)
        kn = pl.multiple_of(n * BLK, BLK)
        t = lax.dot_general(q, k_ref[pl.ds(kn, BLK), :], NT_DIMS, preferred_element_type=F32) * scale + rel
        t = jnp.where(picked > 0.5, t, NEG)
        shift = -slope * ((own - n) * BLK).astype(F32)
        _softmax_step(t, shift, v_ref[pl.ds(kn, BLK), :], m_ref, l_ref, acc_ref, 0)
        return carry

    lax.fori_loop(0, own, past_block, 0)
    o_ref[...] = (acc_ref[0] / l_ref[0]).astype(o_ref.dtype)


def _moba_attention(zb, slopes, batch):
    T = zb.shape[0]
    S = T // batch
    H = MOBA_HEADS
    BLK = MOBA_BLOCK
    assert S % BLK == 0
    nb = S // BLK
    assert nb <= LANES
    smem = pl.BlockSpec(memory_space=pltpu.SMEM)
    return pl.pallas_call(
        functools.partial(_moba_body, nb=nb),
        grid=(batch, H, nb),
        in_specs=[smem,
                  pl.BlockSpec((BLK, HEAD_DIM), lambda b, h, i: (b * nb + i, h)),
                  pl.BlockSpec((S, HEAD_DIM), lambda b, h, i: (b, H + h)),
                  pl.BlockSpec((S, HEAD_DIM), lambda b, h, i: (b, 2 * H + h))],
        out_specs=pl.BlockSpec((BLK, HEAD_DIM), lambda b, h, i: (b * nb + i, h)),
        out_shape=jax.ShapeDtypeStruct((T, H * HEAD_DIM), BF16),
        scratch_shapes=[pltpu.VMEM((LANES, HEAD_DIM), F32),
                        pltpu.VMEM((1, BLK, 1), F32), pltpu.VMEM((1, BLK, 1), F32),
                        pltpu.VMEM((1, BLK, HEAD_DIM), F32)],
        compiler_params=_params("parallel", "parallel", "arbitrary"),
        name="moba_attention",
    )(slopes, zb, zb, zb)


def _log_sigmoid(x):
    return jnp.minimum(x, 0.0) - jnp.log(1.0 + jnp.exp(-jnp.abs(x)))


def _split3(x):
    x1 = x.astype(BF16)
    r1 = x - x1.astype(F32)
    x2 = r1.astype(BF16)
    x3 = (r1 - x2.astype(F32)).astype(BF16)
    return x1, x2, x3


def _mlstm_body(qk_ref, v_ref, og_ref, ifc_ref, ifr_ref, cw_ref, cb_ref, gbc_ref, gbr_ref, ng_ref, out_ref,
                xext_ref, ct_ref, n_ref, m_ref, *, L):
    H, DK, DV = ML_HEADS, HEAD_DIM, ML_V
    PAD = SUBLANES
    c = pl.program_id(1)

    @pl.when(c == 0)
    def _():
        xext_ref[0:PAD, :] = jnp.zeros((PAD, xext_ref.shape[1]), F32)
        ct_ref[...] = jnp.zeros(ct_ref.shape, F32)
        n_ref[...] = jnp.zeros(n_ref.shape, F32)
        m_ref[...] = jnp.zeros(m_ref.shape, F32)

    xext_ref[PAD:PAD + L, :] = qk_ref[...].astype(F32)

    def conv_silu(col0):
        cs = slice(col0, col0 + DK)
        y = cb_ref[:, cs] + xext_ref[PAD:PAD + L, cs] * cw_ref[CONV_W - 1:CONV_W, cs]
        for j in range(CONV_W - 1):
            off = PAD - (CONV_W - 1) + j
            y = y + xext_ref[off:off + L, cs] * cw_ref[j:j + 1, cs]
        return y * jax.nn.sigmoid(y)

    rt = lax.broadcasted_iota(jnp.int32, (L, L), 0)
    cl = lax.broadcasted_iota(jnp.int32, (L, L), 1)
    tril = rt >= cl
    ones_tril = jnp.where(tril, 1.0, 0.0).astype(BF16)
    strict = rt > cl

    for h in range(H):
        qh = conv_silu(h * DK)
        kh = conv_silu(H * DK + h * DK) * (DK ** -0.5)
        qb = qh.astype(BF16)
        kb = kh.astype(BF16)
        vh = v_ref[:, h * DV:(h + 1) * DV]
        i_col = ifc_ref[:, h:h + 1] + gbc_ref[:, h:h + 1]
        lf_col = _log_sigmoid(ifc_ref[:, H + h:H + h + 1] + gbc_ref[:, H + h:H + h + 1])
        i_row = ifr_ref[h:h + 1, :] + gbr_ref[h:h + 1, :]
        b1, b2, b3 = _split3(jnp.where(strict, lf_col, 0.0))
        dp = (jnp.dot(ones_tril, b1, preferred_element_type=F32)
              + jnp.dot(ones_tril, b2, preferred_element_type=F32)
              + jnp.dot(ones_tril, b3, preferred_element_type=F32))
        g_col = dp[:, 0:1] + lf_col[0:1, :]
        g_last = g_col[L - 1:L, :]
        m_prev = m_ref[h][:, 0:1]
        d = jnp.where(tril, dp + i_row, NEG)
        inter = g_col + m_prev
        m_t = jnp.maximum(inter, jnp.max(d, axis=-1, keepdims=True))
        w = jnp.exp(d - m_t)
        a = jnp.exp(inter - m_t)
        sw = lax.dot_general(qb, kb, NT_DIMS, preferred_element_type=F32) * w
        num = (a * jnp.dot(qb, ct_ref[h].astype(BF16), preferred_element_type=F32)
               + jnp.dot(sw.astype(BF16), vh, preferred_element_type=F32))
        den = a * jnp.sum(qh * n_ref[h], axis=-1, keepdims=True) + jnp.sum(sw, axis=-1, keepdims=True)
        hh = num / jnp.maximum(jnp.abs(den), jnp.exp(-m_t))
        a_last = g_last - g_col + i_col
        m_new = jnp.maximum(g_last + m_prev, jnp.max(a_last, axis=0, keepdims=True))
        kw = kh * jnp.exp(a_last - m_new)
        decay = jnp.exp(g_last + m_prev - m_new)
        ct_ref[h] = decay * ct_ref[h] + lax.dot_general(kw.astype(BF16), vh, TN_DIMS, preferred_element_type=F32)
        n_ref[h] = decay * n_ref[h] + jnp.sum(kw, axis=0, keepdims=True)
        m_ref[h] = jnp.broadcast_to(m_new, (1, LANES))
        mu = jnp.mean(hh, axis=-1, keepdims=True)
        xc = hh - mu
        var = jnp.mean(xc * xc, axis=-1, keepdims=True)
        vs = slice(h * DV, (h + 1) * DV)
        yn = xc * lax.rsqrt(var + EPS) * ng_ref[:, vs]
        out_ref[:, vs] = (yn * jax.nn.sigmoid(og_ref[:, vs].astype(F32))).astype(out_ref.dtype)

    xext_ref[0:PAD, :] = xext_ref[L:L + PAD, :]


def _mlstm(zm, zif, conv_w, conv_b, gate_b, norm_g, batch, L=256):
    T = zm.shape[0]
    S = T // batch
    H = ML_HEADS
    L = min(L, S)
    nc = S // L
    W = 2 * H * HEAD_DIM
    zif_row = zif.reshape(batch, S, 2 * H).transpose(0, 2, 1)
    gb = gate_b.astype(F32).reshape(2 * H)
    row = lambda b, c: (b * nc + c, 0)
    return pl.pallas_call(
        functools.partial(_mlstm_body, L=L),
        grid=(batch, nc),
        in_specs=[pl.BlockSpec((L, W), row),
                  pl.BlockSpec((L, W), lambda b, c: (b * nc + c, 1)),
                  pl.BlockSpec((L, W), lambda b, c: (b * nc + c, 2)),
                  pl.BlockSpec((L, 2 * H), row),
                  pl.BlockSpec((None, 2 * H, L), lambda b, c: (b, 0, c)),
                  pl.BlockSpec((CONV_W, W), lambda b, c: (0, 0)),
                  pl.BlockSpec((1, W), lambda b, c: (0, 0)),
                  pl.BlockSpec((1, 2 * H), lambda b, c: (0, 0)),
                  pl.BlockSpec((2 * H, 1), lambda b, c: (0, 0)),
                  pl.BlockSpec((1, W), lambda b, c: (0, 0))],
        out_specs=pl.BlockSpec((L, W), row),
        out_shape=jax.ShapeDtypeStruct((T, W), BF16),
        scratch_shapes=[pltpu.VMEM((L + 2 * SUBLANES, W), F32),
                        pltpu.VMEM((H, HEAD_DIM, ML_V), F32),
                        pltpu.VMEM((H, 1, HEAD_DIM), F32),
                        pltpu.VMEM((H, 1, LANES), F32)],
        compiler_params=_params("parallel", "arbitrary"),
        name="mlstm",
    )(zm, zm, zm, zif, zif_row, conv_w.astype(F32), conv_b.reshape(1, W).astype(F32),
      gb.reshape(1, 2 * H), gb.reshape(2 * H, 1), norm_g.reshape(1, W).astype(F32))


def _alibi_slopes(n):
    return jnp.asarray(2.0 ** (-8.0 * jnp.arange(1, n + 1, dtype=F32) / n), dtype=F32)


def _mixer(xf, batch, layer_idx, norm1_g, w_in, diff_lambda, diff_norm_g, ml_conv_w, ml_conv_b, ml_gate_b,
           ml_norm_g, w_branch, w_out):
    D = xf.shape[1]
    BW = DIFF_HEADS * DIFF_V
    h = _rmsnorm(xf, norm1_g, BF16)
    n_qkv = 3 * BW
    c0, c1 = n_qkv, 2 * n_qkv
    c2 = c1 + 2 * ML_HEADS
    c3 = c2 + n_qkv
    zd = _matmul(h, w_in[:, :c0].astype(BF16), BF16, 1024, 512, name="proj_diff")
    zm = _matmul(h, w_in[:, c0:c1].astype(BF16), BF16, 1024, 512, name="proj_mlstm")
    w_if = jnp.pad(w_in[:, c1:c2], ((0, 0), (0, LANES - 2 * ML_HEADS))).astype(BF16)
    zif = _matmul(h, w_if, F32, 1024, LANES, name="proj_if")[:, :2 * ML_HEADS]
    zb = _matmul(h, w_in[:, c2:c3].astype(BF16), BF16, 1024, 512, name="proj_moba")
    zg = _matmul(h, w_in[:, c3:].astype(BF16), BF16, 1024, 512, name="proj_gates")

    lam_init = 0.8 - 0.6 * math.exp(-0.3 * layer_idx)
    ya = _diff_attention(zd, diff_lambda, diff_norm_g, _alibi_slopes(DIFF_HEADS), batch, lam_init)
    yb = _mlstm(zm, zif, ml_conv_w, ml_conv_b, ml_gate_b, ml_norm_g, batch)
    yc = _moba_attention(zb, _alibi_slopes(MOBA_HEADS), batch)
    merged = _merge(ya, yb, yc, w_branch.astype(BF16), zg)
    return _matmul(merged, w_out.astype(BF16), F32, 1024, 512, residual=xf, name="proj_out")


def _ffn(xf, norm2_g, w_gate_up, w_down):
    h = _rmsnorm(xf, norm2_g, BF16)
    act = _swiglu_up(h, w_gate_up.astype(BF16))
    return _matmul(act, w_down.astype(BF16), F32, 512, 256, residual=xf, name="ffn_down")


def kernel(x, norm1_g, w_in, diff_lambda, diff_norm_g, ml_conv_w, ml_conv_b, ml_gate_b, ml_norm_g, w_branch, w_out,
           norm2_g, w_gate_up, w_down, final_g):
    B, S, D = x.shape
    xf = x.reshape(B * S, D)
    for l in range(w_in.shape[0]):
        xf = _mixer(xf, B, l, norm1_g[l], w_in[l], diff_lambda[l], diff_norm_g[l], ml_conv_w[l], ml_conv_b[l],
                    ml_gate_b[l], ml_norm_g[l], w_branch[l], w_out[l])
        xf = _ffn(xf, norm2_g[l], w_gate_up[l], w_down[l])
    return _rmsnorm(xf, final_g, F32).reshape(B, S, D)
```

```python
import functools
import math

import jax
import jax.numpy as jnp
from jax import lax
from jax.experimental import pallas as pl
from jax.experimental.pallas import tpu as pltpu

F32 = jnp.float32
BF16 = jnp.bfloat16

HEAD_DIM = 128
DIFF_HEADS = 8
DIFF_V = 2 * HEAD_DIM
ML_HEADS = 8
ML_V = 2 * HEAD_DIM
CONV_W = 4
MOBA_HEADS = 16
MOBA_BLOCK = 256
MOBA_TOPK = 3
N_BRANCH = 3
EPS = 1e-6
NEG = -1e30
LOG2E = math.log2(math.e)

LANES = 128
SUBLANES = 8
VMEM_LIMIT = 48 * 1024 * 1024

ATTN_TQ = 256
ATTN_TK = 512

NT_DIMS = (((1,), (1,)), ((), ()))
TN_DIMS = (((0,), (0,)), ((), ()))


def _params(*sem):
    return pltpu.CompilerParams(dimension_semantics=sem, vmem_limit_bytes=VMEM_LIMIT)


def _rmsnorm_body(x_ref, g_ref, o_ref):
    x = x_ref[...]
    ms = jnp.mean(x * x, axis=-1, keepdims=True)
    o_ref[...] = (x * lax.rsqrt(ms + EPS) * g_ref[...]).astype(o_ref.dtype)


def _rmsnorm(x, g, out_dtype, tr=256):
    T, D = x.shape
    tr = min(tr, T)
    return pl.pallas_call(
        _rmsnorm_body,
        grid=(T // tr,),
        in_specs=[pl.BlockSpec((tr, D), lambda i: (i, 0)), pl.BlockSpec((1, D), lambda i: (0, 0))],
        out_specs=pl.BlockSpec((tr, D), lambda i: (i, 0)),
        out_shape=jax.ShapeDtypeStruct((T, D), out_dtype),
        compiler_params=_params("parallel"),
        name="rmsnorm",
    )(x, g.reshape(1, D).astype(F32))


def _mm_body(a_ref, b_ref, o_ref):
    o_ref[...] = jnp.dot(a_ref[...], b_ref[...], preferred_element_type=F32).astype(o_ref.dtype)


def _mm_res_body(a_ref, b_ref, r_ref, o_ref):
    o_ref[...] = (r_ref[...] + jnp.dot(a_ref[...], b_ref[...], preferred_element_type=F32)).astype(o_ref.dtype)


def _matmul(a, b, out_dtype, tm, tn, residual=None, name="matmul"):
    M, K = a.shape
    N = b.shape[1]
    tm, tn = min(tm, M), min(tn, N)
    assert M % tm == 0 and N % tn == 0
    in_specs = [pl.BlockSpec((tm, K), lambda i, j: (i, 0)), pl.BlockSpec((K, tn), lambda i, j: (0, j))]
    args = [a, b]
    body = _mm_body
    if residual is not None:
        in_specs.append(pl.BlockSpec((tm, tn), lambda i, j: (i, j)))
        args.append(residual)
        body = _mm_res_body
    return pl.pallas_call(
        body,
        grid=(M // tm, N // tn),
        in_specs=in_specs,
        out_specs=pl.BlockSpec((tm, tn), lambda i, j: (i, j)),
        out_shape=jax.ShapeDtypeStruct((M, N), out_dtype),
        compiler_params=_params("parallel", "parallel"),
        name=name,
    )(*args)


def _glu_body(a_ref, wg_ref, wu_ref, o_ref):
    a = a_ref[...]
    g = jnp.dot(a, wg_ref[...], preferred_element_type=F32)
    u = jnp.dot(a, wu_ref[...], preferred_element_type=F32)
    o_ref[...] = (g * jax.nn.sigmoid(g) * u).astype(o_ref.dtype)


def _swiglu_up(a, w_gu, tm=1024, tn=256):
    M, K = a.shape
    F = w_gu.shape[1] // 2
    tm = min(tm, M)
    nj = F // tn
    assert F % tn == 0 and M % tm == 0
    return pl.pallas_call(
        _glu_body,
        grid=(M // tm, nj),
        in_specs=[pl.BlockSpec((tm, K), lambda i, j: (i, 0)),
                  pl.BlockSpec((K, tn), lambda i, j: (0, j)),
                  pl.BlockSpec((K, tn), lambda i, j: (0, nj + j))],
        out_specs=pl.BlockSpec((tm, tn), lambda i, j: (i, j)),
        out_shape=jax.ShapeDtypeStruct((M, F), BF16),
        compiler_params=_params("parallel", "parallel"),
        name="swiglu_up",
    )(a, w_gu, w_gu)


def _merge_body(ya_ref, yb_ref, yc_ref, w0_ref, w1_ref, w2_ref, g0_ref, g1_ref, g2_ref, o_ref):
    acc = jax.nn.sigmoid(g0_ref[...].astype(F32)) * jnp.dot(ya_ref[...], w0_ref[...], preferred_element_type=F32)
    acc += jax.nn.sigmoid(g1_ref[...].astype(F32)) * jnp.dot(yb_ref[...], w1_ref[...], preferred_element_type=F32)
    acc += jax.nn.sigmoid(g2_ref[...].astype(F32)) * jnp.dot(yc_ref[...], w2_ref[...], preferred_element_type=F32)
    o_ref[...] = acc.astype(o_ref.dtype)


def _merge(ya, yb, yc, w_branch, gates, tm=512, tn=512):
    M, K = ya.shape
    D = w_branch.shape[2]
    tm, tn = min(tm, M), min(tn, D)
    nj = D // tn
    y_spec = pl.BlockSpec((tm, K), lambda i, j: (i, 0))
    w_specs = [pl.BlockSpec((None, K, tn), functools.partial(lambda i, j, b: (b, 0, j), b=b)) for b in range(N_BRANCH)]
    g_specs = [pl.BlockSpec((tm, tn), functools.partial(lambda i, j, b: (i, b * nj + j), b=b)) for b in range(N_BRANCH)]
    return pl.pallas_call(
        _merge_body,
        grid=(M // tm, nj),
        in_specs=[y_spec, y_spec, y_spec] + w_specs + g_specs,
        out_specs=pl.BlockSpec((tm, tn), lambda i, j: (i, j)),
        out_shape=jax.ShapeDtypeStruct((M, D), BF16),
        compiler_params=_params("parallel", "parallel"),
        name="merge",
    )(ya, yb, yc, w_branch, w_branch, w_branch, gates, gates, gates)


def _softmax_step(t, shift, vt, m_prev, l_prev, acc_ref, idx):
    m_new = jnp.maximum(m_prev, jnp.max(t, axis=0, keepdims=True) + shift)
    p = jnp.exp2(t - (m_new - shift))
    alpha = jnp.exp2(m_prev - m_new)
    l_new = alpha * l_prev + jnp.sum(p, axis=0, keepdims=True)
    acc_ref[idx] = alpha * acc_ref[idx] + jnp.dot(vt, p.astype(BF16), preferred_element_type=F32)
    return m_new, l_new


def _tile_alibi(nk, nq, slope):
    kpos = lax.broadcasted_iota(jnp.int32, (nk, nq), 0)
    qpos = lax.broadcasted_iota(jnp.int32, (nk, nq), 1)
    off = kpos - qpos
    return off, off.astype(F32) * slope


def _diff_body(slopes_ref, lam_ref, q_ref, k_ref, vt_ref, g_ref, o_ref, acc_ref, *, tq, tk, lam_init):
    h = pl.program_id(1)
    qi = pl.program_id(2)
    slope = slopes_ref[h] * LOG2E
    scale = HEAD_DIM ** -0.5 * LOG2E
    lp = lam_ref[...]
    lam = (jnp.exp(jnp.sum(lp[0:1] * lp[1:2], axis=-1, keepdims=True))
           - jnp.exp(jnp.sum(lp[2:3] * lp[3:4], axis=-1, keepdims=True)) + lam_init)
    off, rel = _tile_alibi(tk, tq, slope)
    acc_ref[...] = jnp.zeros(acc_ref.shape, F32)
    q0 = qi * tq

    def step(j, stats, masked):
        k0 = pl.multiple_of(j * tk, tk)
        shift = -slope * (q0 - k0).astype(F32)
        vt = vt_ref[j]
        out = ()
        for c in range(2):
            q = q_ref[:, c * HEAD_DIM:(c + 1) * HEAD_DIM]
            k = k_ref[pl.ds(k0, tk), c * HEAD_DIM:(c + 1) * HEAD_DIM]
            t = lax.dot_general(k, q, NT_DIMS, preferred_element_type=F32) * scale + rel
            if masked:
                t = jnp.where(off <= q0 - k0, t, NEG)
            out += _softmax_step(t, shift, vt, stats[2 * c], stats[2 * c + 1], acc_ref, c)
        return out

    n_full = q0 // tk
    m_init = jnp.full((1, tq), NEG, F32)
    l_init = jnp.zeros((1, tq), F32)
    stats = lax.fori_loop(0, n_full, lambda j, s: step(j, s, False), (m_init, l_init, m_init, l_init))
    _, l0, _, l1 = step(n_full, stats, True)

    ot = acc_ref[0] * (1.0 / l0) - acc_ref[1] * (lam * (1.0 / l1))
    o = ot.T
    ms = jnp.mean(o * o, axis=-1, keepdims=True)
    y = (o * lax.rsqrt(ms + EPS) * g_ref[...]) * (1.0 - lam_init)
    o_ref[...] = y.astype(o_ref.dtype)


def _diff_attention(zd, diff_lambda, norm_g, slopes, batch, lam_init, tq=ATTN_TQ, tk=ATTN_TK):
    T = zd.shape[0]
    S = T // batch
    H = DIFF_HEADS
    nq = S // tq
    nk = S // tk
    assert S % tk == 0 and tk % tq == 0
    W = 2 * HEAD_DIM
    vt = zd[:, 2 * H * W:].reshape(batch, nk, tk, H, W).transpose(0, 3, 1, 4, 2)
    smem = pl.BlockSpec(memory_space=pltpu.SMEM)
    return pl.pallas_call(
        functools.partial(_diff_body, tq=tq, tk=tk, lam_init=lam_init),
        grid=(batch, H, nq),
        in_specs=[smem,
                  pl.BlockSpec((4, HEAD_DIM), lambda b, h, i: (0, 0)),
                  pl.BlockSpec((tq, W), lambda b, h, i: (b * nq + i, h)),
                  pl.BlockSpec((S, W), lambda b, h, i: (b, H + h)),
                  pl.BlockSpec((None, None, nk, W, tk), lambda b, h, i: (b, h, 0, 0, 0)),
                  pl.BlockSpec((1, W), lambda b, h, i: (0, 0))],
        out_specs=pl.BlockSpec((tq, W), lambda b, h, i: (b * nq + i, h)),
        out_shape=jax.ShapeDtypeStruct((T, H * W), BF16),
        scratch_shapes=[pltpu.VMEM((2, W, tq), F32)],
        compiler_params=_params("parallel", "parallel", "parallel"),
        name="diff_attention",
    )(slopes, diff_lambda.astype(F32), zd, zd, vt, norm_g.reshape(1, W).astype(F32))


def _moba_body(slopes_ref, q_ref, k_ref, vt_ref, o_ref, km_ref, ch_ref, acc_ref, *, nb, tk):
    h = pl.program_id(1)
    own = pl.program_id(2)
    BLK = MOBA_BLOCK
    nbp = km_ref.shape[0]
    slope = slopes_ref[h] * LOG2E
    scale = HEAD_DIM ** -0.5 * LOG2E

    @pl.when(own == 0)
    def _():
        km_ref[...] = jnp.zeros(km_ref.shape, F32)
        for n in range(nb):
            kb = k_ref[n * BLK:(n + 1) * BLK, :].astype(F32)
            km_ref[n:n + 1, :] = jnp.mean(kb, axis=0, keepdims=True)

    q = q_ref[...]
    km = km_ref[...]
    km_hi = km.astype(BF16)
    km_lo = (km - km_hi.astype(F32)).astype(BF16)
    gs = (lax.dot_general(km_hi, q, NT_DIMS, preferred_element_type=F32)
          + lax.dot_general(km_lo, q, NT_DIMS, preferred_element_type=F32))
    blk = lax.broadcasted_iota(jnp.int32, (nbp, BLK), 0)
    rank = jnp.zeros((nbp, BLK), F32)
    for m in range(nb):
        row = gs[m:m + 1, :]
        tie = jnp.where(blk > m, 1.0, 0.0)
        beats = jnp.where(row > gs, 1.0, jnp.where(row == gs, tie, 0.0))
        rank = rank + beats * jnp.where(m < own, 1.0, 0.0)
    chosen = jnp.where(rank < float(min(MOBA_TOPK, nb)), 1.0, 0.0)
    ch_ref[...] = jnp.where(blk < own, chosen, jnp.where(blk == own, 1.0, 0.0))

    bpt = tk // BLK
    off, rel = _tile_alibi(tk, BLK, slope)
    acc_ref[...] = jnp.zeros(acc_ref.shape, F32)
    q0 = own * BLK

    def step(j, stats, masked):
        k0 = pl.multiple_of(j * tk, tk)
        t = lax.dot_general(k_ref[pl.ds(k0, tk), :], q, NT_DIMS, preferred_element_type=F32) * scale + rel
        parts = []
        for i in range(bpt):
            allowed = ch_ref[pl.ds(j * bpt + i, 1), :]
            parts.append(jnp.where(allowed > 0.5, t[i * BLK:(i + 1) * BLK], NEG))
        t = jnp.concatenate(parts, axis=0)
        if masked:
            t = jnp.where(off <= q0 - k0, t, NEG)
        shift = -slope * (q0 - k0).astype(F32)
        return _softmax_step(t, shift, vt_ref[j], stats[0], stats[1], acc_ref, 0)

    n_full = q0 // tk
    stats = (jnp.full((1, BLK), NEG, F32), jnp.zeros((1, BLK), F32))
    stats = lax.fori_loop(0, n_full, lambda j, s: step(j, s, False), stats)
    _, l = step(n_full, stats, True)
    o_ref[...] = (acc_ref[0] * (1.0 / l)).T.astype(o_ref.dtype)


def _moba_attention(zb, slopes, batch, tk=ATTN_TK):
    T = zb.shape[0]
    S = T // batch
    H = MOBA_HEADS
    BLK = MOBA_BLOCK
    assert S % tk == 0 and tk % BLK == 0
    nb = S // BLK
    nk = S // tk
    nbp = -(-nb // 16) * 16
    vt = zb[:, 2 * H * HEAD_DIM:].reshape(batch, nk, tk, H, HEAD_DIM).transpose(0, 3, 1, 4, 2)
    smem = pl.BlockSpec(memory_space=pltpu.SMEM)
    return pl.pallas_call(
        functools.partial(_moba_body, nb=nb, tk=tk),
        grid=(batch, H, nb),
        in_specs=[smem,
                  pl.BlockSpec((BLK, HEAD_DIM), lambda b, h, i: (b * nb + i, h)),
                  pl.BlockSpec((S, HEAD_DIM), lambda b, h, i: (b, H + h)),
                  pl.BlockSpec((None, None, nk, HEAD_DIM, tk), lambda b, h, i: (b, h, 0, 0, 0))],
        out_specs=pl.BlockSpec((BLK, HEAD_DIM), lambda b, h, i: (b * nb + i, h)),
        out_shape=jax.ShapeDtypeStruct((T, H * HEAD_DIM), BF16),
        scratch_shapes=[pltpu.VMEM((nbp, HEAD_DIM), F32), pltpu.VMEM((nbp, BLK), F32),
                        pltpu.VMEM((1, HEAD_DIM, BLK), F32)],
        compiler_params=_params("parallel", "parallel", "arbitrary"),
        name="moba_attention",
    )(slopes, zb, zb, vt)


def _log_sigmoid(x):
    return jnp.minimum(x, 0.0) - jnp.log(1.0 + jnp.exp(-jnp.abs(x)))


def _split3(x):
    x1 = x.astype(BF16)
    r1 = x - x1.astype(F32)
    x2 = r1.astype(BF16)
    x3 = (r1 - x2.astype(F32)).astype(BF16)
    return x1, x2, x3


def _mlstm_body(qk_ref, v_ref, og_ref, ifc_ref, ifr_ref, cw_ref, cb_ref, gbc_ref, gbr_ref, ng_ref, out_ref,
                xext_ref, ct_ref, n_ref, m_ref, *, L):
    H, DK, DV = ML_HEADS, HEAD_DIM, ML_V
    PAD = SUBLANES
    c = pl.program_id(1)

    @pl.when(c == 0)
    def _():
        xext_ref[0:PAD, :] = jnp.zeros((PAD, xext_ref.shape[1]), F32)
        ct_ref[...] = jnp.zeros(ct_ref.shape, F32)
        n_ref[...] = jnp.zeros(n_ref.shape, F32)
        m_ref[...] = jnp.zeros(m_ref.shape, F32)

    xext_ref[PAD:PAD + L, :] = qk_ref[...].astype(F32)

    def conv_silu(col0):
        cs = slice(col0, col0 + DK)
        y = cb_ref[:, cs] + xext_ref[PAD:PAD + L, cs] * cw_ref[CONV_W - 1:CONV_W, cs]
        for j in range(CONV_W - 1):
            off = PAD - (CONV_W - 1) + j
            y = y + xext_ref[off:off + L, cs] * cw_ref[j:j + 1, cs]
        return y * jax.nn.sigmoid(y)

    rt = lax.broadcasted_iota(jnp.int32, (L, L), 0)
    cl = lax.broadcasted_iota(jnp.int32, (L, L), 1)
    tril = rt >= cl
    ones_tril = jnp.where(tril, 1.0, 0.0).astype(BF16)
    strict = rt > cl

    for h in range(H):
        qh = conv_silu(h * DK)
        kh = conv_silu(H * DK + h * DK) * (DK ** -0.5)
        qb = qh.astype(BF16)
        kb = kh.astype(BF16)
        vh = v_ref[:, h * DV:(h + 1) * DV]
        i_col = ifc_ref[:, h:h + 1] + gbc_ref[:, h:h + 1]
        lf_col = _log_sigmoid(ifc_ref[:, H + h:H + h + 1] + gbc_ref[:, H + h:H + h + 1])
        i_row = ifr_ref[h:h + 1, :] + gbr_ref[h:h + 1, :]
        b1, b2, b3 = _split3(jnp.where(strict, lf_col, 0.0))
        dp = (jnp.dot(ones_tril, b1, preferred_element_type=F32)
              + jnp.dot(ones_tril, b2, preferred_element_type=F32)
              + jnp.dot(ones_tril, b3, preferred_element_type=F32))
        g_col = dp[:, 0:1] + lf_col[0:1, :]
        g_last = g_col[L - 1:L, :]
        m_prev = m_ref[h][:, 0:1]
        d = jnp.where(tril, dp + i_row, NEG)
        inter = g_col + m_prev
        m_t = jnp.maximum(inter, jnp.max(d, axis=-1, keepdims=True))
        w = jnp.exp(d - m_t)
        a = jnp.exp(inter - m_t)
        sw = lax.dot_general(qb, kb, NT_DIMS, preferred_element_type=F32) * w
        num = (a * jnp.dot(qb, ct_ref[h].astype(BF16), preferred_element_type=F32)
               + jnp.dot(sw.astype(BF16), vh, preferred_element_type=F32))
        den = a * jnp.sum(qh * n_ref[h], axis=-1, keepdims=True) + jnp.sum(sw, axis=-1, keepdims=True)
        hh = num / jnp.maximum(jnp.abs(den), jnp.exp(-m_t))
        a_last = g_last - g_col + i_col
        m_new = jnp.maximum(g_last + m_prev, jnp.max(a_last, axis=0, keepdims=True))
        kw = kh * jnp.exp(a_last - m_new)
        decay = jnp.exp(g_last + m_prev - m_new)
        ct_ref[h] = decay * ct_ref[h] + lax.dot_general(kw.astype(BF16), vh, TN_DIMS, preferred_element_type=F32)
        n_ref[h] = decay * n_ref[h] + jnp.sum(kw, axis=0, keepdims=True)
        m_ref[h] = jnp.broadcast_to(m_new, (1, LANES))
        mu = jnp.mean(hh, axis=-1, keepdims=True)
        xc = hh - mu
        var = jnp.mean(xc * xc, axis=-1, keepdims=True)
        vs = slice(h * DV, (h + 1) * DV)
        yn = xc * lax.rsqrt(var + EPS) * ng_ref[:, vs]
        out_ref[:, vs] = (yn * jax.nn.sigmoid(og_ref[:, vs].astype(F32))).astype(out_ref.dtype)

    xext_ref[0:PAD, :] = xext_ref[L:L + PAD, :]


def _mlstm(zm, zif, conv_w, conv_b, gate_b, norm_g, batch, L=256):
    T = zm.shape[0]
    S = T // batch
    H = ML_HEADS
    L = min(L, S)
    nc = S // L
    W = 2 * H * HEAD_DIM
    zif_row = zif.reshape(batch, S, 2 * H).transpose(0, 2, 1)
    gb = gate_b.astype(F32).reshape(2 * H)
    row = lambda b, c: (b * nc + c, 0)
    return pl.pallas_call(
        functools.partial(_mlstm_body, L=L),
        grid=(batch, nc),
        in_specs=[pl.BlockSpec((L, W), row),
                  pl.BlockSpec((L, W), lambda b, c: (b * nc + c, 1)),
                  pl.BlockSpec((L, W), lambda b, c: (b * nc + c, 2)),
                  pl.BlockSpec((L, 2 * H), row),
                  pl.BlockSpec((None, 2 * H, L), lambda b, c: (b, 0, c)),
                  pl.BlockSpec((CONV_W, W), lambda b, c: (0, 0)),
                  pl.BlockSpec((1, W), lambda b, c: (0, 0)),
                  pl.BlockSpec((1, 2 * H), lambda b, c: (0, 0)),
                  pl.BlockSpec((2 * H, 1), lambda b, c: (0, 0)),
                  pl.BlockSpec((1, W), lambda b, c: (0, 0))],
        out_specs=pl.BlockSpec((L, W), row),
        out_shape=jax.ShapeDtypeStruct((T, W), BF16),
        scratch_shapes=[pltpu.VMEM((L + 2 * SUBLANES, W), F32),
                        pltpu.VMEM((H, HEAD_DIM, ML_V), F32),
                        pltpu.VMEM((H, 1, HEAD_DIM), F32),
                        pltpu.VMEM((H, 1, LANES), F32)],
        compiler_params=_params("parallel", "arbitrary"),
        name="mlstm",
    )(zm, zm, zm, zif, zif_row, conv_w.astype(F32), conv_b.reshape(1, W).astype(F32),
      gb.reshape(1, 2 * H), gb.reshape(2 * H, 1), norm_g.reshape(1, W).astype(F32))


def _alibi_slopes(n):
    return jnp.asarray(2.0 ** (-8.0 * jnp.arange(1, n + 1, dtype=F32) / n), dtype=F32)


def _mixer(xf, batch, layer_idx, norm1_g, w_in, diff_lambda, diff_norm_g, ml_conv_w, ml_conv_b, ml_gate_b,
           ml_norm_g, w_branch, w_out):
    D = xf.shape[1]
    BW = DIFF_HEADS * DIFF_V
    h = _rmsnorm(xf, norm1_g, BF16)
    n_qkv = 3 * BW
    c0, c1 = n_qkv, 2 * n_qkv
    c2 = c1 + 2 * ML_HEADS
    c3 = c2 + n_qkv
    zd = _matmul(h, w_in[:, :c0].astype(BF16), BF16, 1024, 512, name="proj_diff")
    zm = _matmul(h, w_in[:, c0:c1].astype(BF16), BF16, 1024, 512, name="proj_mlstm")
    w_if = jnp.pad(w_in[:, c1:c2], ((0, 0), (0, LANES - 2 * ML_HEADS))).astype(BF16)
    zif = _matmul(h, w_if, F32, 1024, LANES, name="proj_if")[:, :2 * ML_HEADS]
    zb = _matmul(h, w_in[:, c2:c3].astype(BF16), BF16, 1024, 512, name="proj_moba")
    zg = _matmul(h, w_in[:, c3:].astype(BF16), BF16, 1024, 512, name="proj_gates")

    lam_init = 0.8 - 0.6 * math.exp(-0.3 * layer_idx)
    ya = _diff_attention(zd, diff_lambda, diff_norm_g, _alibi_slopes(DIFF_HEADS), batch, lam_init)
    yb = _mlstm(zm, zif, ml_conv_w, ml_conv_b, ml_gate_b, ml_norm_g, batch)
    yc = _moba_attention(zb, _alibi_slopes(MOBA_HEADS), batch)
    merged = _merge(ya, yb, yc, w_branch.astype(BF16), zg)
    return _matmul(merged, w_out.astype(BF16), F32, 1024, 512, residual=xf, name="proj_out")


def _ffn(xf, norm2_g, w_gate_up, w_down):
    h = _rmsnorm(xf, norm2_g, BF16)
    act = _swiglu_up(h, w_gate_up.astype(BF16))
    return _matmul(act, w_down.astype(BF16), F32, 512, 256, residual=xf, name="ffn_down")


def kernel(x, norm1_g, w_in, diff_lambda, diff_norm_g, ml_conv_w, ml_conv_b, ml_gate_b, ml_norm_g, w_branch, w_out,
           norm2_g, w_gate_up, w_down, final_g):
    B, S, D = x.shape
    xf = x.reshape(B * S, D)
    for l in range(w_in.shape[0]):
        xf = _mixer(xf, B, l, norm1_g[l], w_in[l], diff_lambda[l], diff_norm_g[l], ml_conv_w[l], ml_conv_b[l],
                    ml_gate_b[l], ml_norm_g[l], w_branch[l], w_out[l])
        xf = _ffn(xf, norm2_g[l], w_gate_up[l], w_down[l])
    return _rmsnorm(xf, final_g, F32).reshape(B, S, D)
```

```python
import functools
import math

import jax
import jax.numpy as jnp
from jax import lax
from jax.experimental import pallas as pl
from jax.experimental.pallas import tpu as pltpu

F32 = jnp.float32
BF16 = jnp.bfloat16

HEAD_DIM = 128
DIFF_HEADS = 8
DIFF_V = 2 * HEAD_DIM
ML_HEADS = 8
ML_V = 2 * HEAD_DIM
CONV_W = 4
MOBA_HEADS = 16
MOBA_BLOCK = 256
MOBA_TOPK = 3
N_BRANCH = 3
EPS = 1e-6
NEG = -1e30
LOG2E = math.log2(math.e)

LANES = 128
SUBLANES = 8
VMEM_LIMIT = 48 * 1024 * 1024

ATTN_TQ = 256
ATTN_TK = 512
MOBA_HEADS_PER_STEP = 2

NT_DIMS = (((1,), (1,)), ((), ()))
TN_DIMS = (((0,), (0,)), ((), ()))


def _params(*sem):
    return pltpu.CompilerParams(dimension_semantics=sem, vmem_limit_bytes=VMEM_LIMIT)


def _rmsnorm_body(x_ref, g_ref, o_ref):
    x = x_ref[...]
    ms = jnp.mean(x * x, axis=-1, keepdims=True)
    o_ref[...] = (x * lax.rsqrt(ms + EPS) * g_ref[...]).astype(o_ref.dtype)


def _rmsnorm(x, g, out_dtype, tr=256):
    T, D = x.shape
    tr = min(tr, T)
    return pl.pallas_call(
        _rmsnorm_body,
        grid=(T // tr,),
        in_specs=[pl.BlockSpec((tr, D), lambda i: (i, 0)), pl.BlockSpec((1, D), lambda i: (0, 0))],
        out_specs=pl.BlockSpec((tr, D), lambda i: (i, 0)),
        out_shape=jax.ShapeDtypeStruct((T, D), out_dtype),
        compiler_params=_params("parallel"),
        name="rmsnorm",
    )(x, g.reshape(1, D).astype(F32))


def _mm_body(a_ref, b_ref, o_ref):
    o_ref[...] = jnp.dot(a_ref[...], b_ref[...], preferred_element_type=F32).astype(o_ref.dtype)


def _mm_res_body(a_ref, b_ref, r_ref, o_ref):
    o_ref[...] = (r_ref[...] + jnp.dot(a_ref[...], b_ref[...], preferred_element_type=F32)).astype(o_ref.dtype)


def _matmul(a, b, out_dtype, tm, tn, residual=None, name="matmul"):
    M, K = a.shape
    N = b.shape[1]
    tm, tn = min(tm, M), min(tn, N)
    assert M % tm == 0 and N % tn == 0
    in_specs = [pl.BlockSpec((tm, K), lambda i, j: (i, 0)), pl.BlockSpec((K, tn), lambda i, j: (0, j))]
    args = [a, b]
    body = _mm_body
    if residual is not None:
        in_specs.append(pl.BlockSpec((tm, tn), lambda i, j: (i, j)))
        args.append(residual)
        body = _mm_res_body
    return pl.pallas_call(
        body,
        grid=(M // tm, N // tn),
        in_specs=in_specs,
        out_specs=pl.BlockSpec((tm, tn), lambda i, j: (i, j)),
        out_shape=jax.ShapeDtypeStruct((M, N), out_dtype),
        compiler_params=_params("parallel", "parallel"),
        name=name,
    )(*args)


def _glu_body(a_ref, wg_ref, wu_ref, o_ref):
    a = a_ref[...]
    g = jnp.dot(a, wg_ref[...], preferred_element_type=F32)
    u = jnp.dot(a, wu_ref[...], preferred_element_type=F32)
    o_ref[...] = (g * jax.nn.sigmoid(g) * u).astype(o_ref.dtype)


def _swiglu_up(a, w_gu, tm=1024, tn=256):
    M, K = a.shape
    F = w_gu.shape[1] // 2
    tm = min(tm, M)
    nj = F // tn
    assert F % tn == 0 and M % tm == 0
    return pl.pallas_call(
        _glu_body,
        grid=(M // tm, nj),
        in_specs=[pl.BlockSpec((tm, K), lambda i, j: (i, 0)),
                  pl.BlockSpec((K, tn), lambda i, j: (0, j)),
                  pl.BlockSpec((K, tn), lambda i, j: (0, nj + j))],
        out_specs=pl.BlockSpec((tm, tn), lambda i, j: (i, j)),
        out_shape=jax.ShapeDtypeStruct((M, F), BF16),
        compiler_params=_params("parallel", "parallel"),
        name="swiglu_up",
    )(a, w_gu, w_gu)


def _merge_body(ya_ref, yb_ref, yc_ref, w0_ref, w1_ref, w2_ref, g0_ref, g1_ref, g2_ref, o_ref):
    acc = jax.nn.sigmoid(g0_ref[...].astype(F32)) * jnp.dot(ya_ref[...], w0_ref[...], preferred_element_type=F32)
    acc += jax.nn.sigmoid(g1_ref[...].astype(F32)) * jnp.dot(yb_ref[...], w1_ref[...], preferred_element_type=F32)
    acc += jax.nn.sigmoid(g2_ref[...].astype(F32)) * jnp.dot(yc_ref[...], w2_ref[...], preferred_element_type=F32)
    o_ref[...] = acc.astype(o_ref.dtype)


def _merge(ya, yb, yc, w_branch, gates, tm=512, tn=512):
    M, K = ya.shape
    D = w_branch.shape[2]
    tm, tn = min(tm, M), min(tn, D)
    nj = D // tn
    y_spec = pl.BlockSpec((tm, K), lambda i, j: (i, 0))
    w_specs = [pl.BlockSpec((None, K, tn), functools.partial(lambda i, j, b: (b, 0, j), b=b)) for b in range(N_BRANCH)]
    g_specs = [pl.BlockSpec((tm, tn), functools.partial(lambda i, j, b: (i, b * nj + j), b=b)) for b in range(N_BRANCH)]
    return pl.pallas_call(
        _merge_body,
        grid=(M // tm, nj),
        in_specs=[y_spec, y_spec, y_spec] + w_specs + g_specs,
        out_specs=pl.BlockSpec((tm, tn), lambda i, j: (i, j)),
        out_shape=jax.ShapeDtypeStruct((M, D), BF16),
        compiler_params=_params("parallel", "parallel"),
        name="merge",
    )(ya, yb, yc, w_branch, w_branch, w_branch, gates, gates, gates)


def _softmax_fold(t, tmax, shift, vt, m_prev, l_prev, acc_ref, idx):
    m_new = jnp.maximum(m_prev, tmax + shift)
    p = jnp.exp2(t - (m_new - shift))
    alpha = jnp.exp2(m_prev - m_new)
    l_new = alpha * l_prev + jnp.sum(p, axis=0, keepdims=True)
    acc_ref[idx] = alpha * acc_ref[idx] + jnp.dot(vt, p.astype(BF16), preferred_element_type=F32)
    return m_new, l_new


def _pipelined_tiles(n_last, logits_fn, fold_fn, stats):
    def piped(j, carry, slot):
        stats, maxima = carry
        nxt = logits_fn(j + 1, 1 - slot)
        return fold_fn(j, slot, maxima, stats), nxt

    def body(j, carry):
        return lax.cond((j & 1) == 0, lambda c: piped(j, c, 0), lambda c: piped(j, c, 1), carry)

    carry = lax.fori_loop(0, n_last, body, (stats, logits_fn(0, 0)))
    return lax.cond((n_last & 1) == 0,
                    lambda c: fold_fn(n_last, 0, c[1], c[0]),
                    lambda c: fold_fn(n_last, 1, c[1], c[0]), carry)


def _alibi_bias_tiles(bias_ref, tk, tq, slope):
    off = lax.broadcasted_iota(jnp.int32, (tk, tq), 0) - lax.broadcasted_iota(jnp.int32, (tk, tq), 1)
    rel = off.astype(F32) * slope
    bias_ref[0] = rel
    for r in range(tk // tq):
        bias_ref[1 + r] = jnp.where(off <= r * tq, rel, NEG)


def _diff_body(slopes_ref, lam_ref, q_ref, k_ref, vt_ref, g_ref, o_ref, acc_ref, s_ref, bias_ref, *, tq, tk,
               lam_init):
    h = pl.program_id(1)
    qi = pl.program_id(2)
    slope = slopes_ref[h] * LOG2E
    scale = HEAD_DIM ** -0.5 * LOG2E
    lp = lam_ref[...]
    lam = (jnp.exp(jnp.sum(lp[0:1] * lp[1:2], axis=-1, keepdims=True))
           - jnp.exp(jnp.sum(lp[2:3] * lp[3:4], axis=-1, keepdims=True)) + lam_init)
    q0 = qi * tq
    n_last = q0 // tk
    last_bias = 1 + (q0 - n_last * tk) // tq

    @pl.when(qi == 0)
    def _():
        _alibi_bias_tiles(bias_ref, tk, tq, slope)

    acc_ref[...] = jnp.zeros(acc_ref.shape, F32)

    def logits(j, slot):
        k0 = pl.multiple_of(j * tk, tk)
        bias = bias_ref[jnp.where(j == n_last, last_bias, 0)]
        maxima = ()
        for c in range(2):
            q = q_ref[:, c * HEAD_DIM:(c + 1) * HEAD_DIM]
            k = k_ref[pl.ds(k0, tk), c * HEAD_DIM:(c + 1) * HEAD_DIM]
            t = lax.dot_general(k, q, NT_DIMS, preferred_element_type=F32) * scale + bias
            s_ref[slot, c] = t
            maxima += (jnp.max(t, axis=0, keepdims=True),)
        return maxima

    def fold(j, slot, maxima, stats):
        shift = -slope * (q0 - j * tk).astype(F32)
        vt = vt_ref[j]
        out = ()
        for c in range(2):
            out += _softmax_fold(s_ref[slot, c], maxima[c], shift, vt, stats[2 * c], stats[2 * c + 1], acc_ref, c)
        return out

    m_init = jnp.full((1, tq), NEG, F32)
    l_init = jnp.zeros((1, tq), F32)
    _, l0, _, l1 = _pipelined_tiles(n_last, logits, fold, (m_init, l_init, m_init, l_init))

    ot = acc_ref[0] * (1.0 / l0) - acc_ref[1] * (lam * (1.0 / l1))
    o = ot.T
    ms = jnp.mean(o * o, axis=-1, keepdims=True)
    y = (o * lax.rsqrt(ms + EPS) * g_ref[...]) * (1.0 - lam_init)
    o_ref[...] = y.astype(o_ref.dtype)


def _diff_attention(zd, diff_lambda, norm_g, slopes, batch, lam_init, tq=ATTN_TQ, tk=ATTN_TK):
    T = zd.shape[0]
    S = T // batch
    H = DIFF_HEADS
    nq = S // tq
    nk = S // tk
    assert S % tk == 0 and tk % tq == 0
    W = 2 * HEAD_DIM
    vt = zd[:, 2 * H * W:].reshape(batch, nk, tk, H, W).transpose(0, 3, 1, 4, 2)
    smem = pl.BlockSpec(memory_space=pltpu.SMEM)
    return pl.pallas_call(
        functools.partial(_diff_body, tq=tq, tk=tk, lam_init=lam_init),
        grid=(batch, H, nq),
        in_specs=[smem,
                  pl.BlockSpec((4, HEAD_DIM), lambda b, h, i: (0, 0)),
                  pl.BlockSpec((tq, W), lambda b, h, i: (b * nq + i, h)),
                  pl.BlockSpec((S, W), lambda b, h, i: (b, H + h)),
                  pl.BlockSpec((None, None, nk, W, tk), lambda b, h, i: (b, h, 0, 0, 0)),
                  pl.BlockSpec((1, W), lambda b, h, i: (0, 0))],
        out_specs=pl.BlockSpec((tq, W), lambda b, h, i: (b * nq + i, h)),
        out_shape=jax.ShapeDtypeStruct((T, H * W), BF16),
        scratch_shapes=[pltpu.VMEM((2, W, tq), F32), pltpu.VMEM((2, 2, tk, tq), F32),
                        pltpu.VMEM((1 + tk // tq, tk, tq), F32)],
        compiler_params=_params("parallel", "parallel", "arbitrary"),
        name="diff_attention",
    )(slopes, diff_lambda.astype(F32), zd, zd, vt, norm_g.reshape(1, W).astype(F32))


def _moba_allowed(q, km, own, nb):
    nbp, nq = km.shape[0], q.shape[0]
    km_hi = km.astype(BF16)
    km_lo = (km - km_hi.astype(F32)).astype(BF16)
    gs = (lax.dot_general(km_hi, q, NT_DIMS, preferred_element_type=F32)
          + lax.dot_general(km_lo, q, NT_DIMS, preferred_element_type=F32))
    blk = lax.broadcasted_iota(jnp.int32, (nbp, nq), 0)
    rank = jnp.zeros((nbp, nq), F32)
    for m in range(nb):
        row = gs[m:m + 1, :]
        tie = jnp.where(blk > m, 1.0, 0.0)
        beats = jnp.where(row > gs, 1.0, jnp.where(row == gs, tie, 0.0))
        rank = rank + beats * jnp.where(m < own, 1.0, 0.0)
    chosen = jnp.where(rank < float(min(MOBA_TOPK, nb)), 1.0, 0.0)
    return jnp.where(blk < own, chosen, jnp.where(blk == own, 1.0, 0.0))


def _moba_body(slopes_ref, q_ref, k_ref, vt_ref, o_ref, km_ref, ch_ref, acc_ref, s_ref, bias_ref, *, nb, tk, G):
    hp = pl.program_id(1)
    own = pl.program_id(2)
    BLK = MOBA_BLOCK
    bpt = tk // BLK
    scale = HEAD_DIM ** -0.5 * LOG2E
    slopes = [slopes_ref[hp * G + g] * LOG2E for g in range(G)]
    cols = [slice(g * HEAD_DIM, (g + 1) * HEAD_DIM) for g in range(G)]
    q0 = own * BLK
    n_last = q0 // tk
    last_bias = 1 + (own - n_last * bpt)

    @pl.when(own == 0)
    def _():
        km_ref[...] = jnp.zeros(km_ref.shape, F32)
        for g in range(G):
            _alibi_bias_tiles(bias_ref.at[g], tk, BLK, slopes[g])
            for n in range(nb):
                kb = k_ref[n * BLK:(n + 1) * BLK, cols[g]].astype(F32)
                km_ref[g, n:n + 1, :] = jnp.mean(kb, axis=0, keepdims=True)

    for g in range(G):
        ch_ref[g] = _moba_allowed(q_ref[:, cols[g]], km_ref[g], own, nb)
    acc_ref[...] = jnp.zeros(acc_ref.shape, F32)

    def logits(j, slot):
        k0 = pl.multiple_of(j * tk, tk)
        bidx = jnp.where(j == n_last, last_bias, 0)
        maxima = ()
        for g in range(G):
            t = lax.dot_general(k_ref[pl.ds(k0, tk), cols[g]], q_ref[:, cols[g]], NT_DIMS,
                                preferred_element_type=F32) * scale + bias_ref[g, bidx]
            parts = []
            for i in range(bpt):
                allowed = ch_ref[g, pl.ds(j * bpt + i, 1), :]
                parts.append(jnp.where(allowed > 0.5, t[i * BLK:(i + 1) * BLK], NEG))
            t = jnp.concatenate(parts, axis=0)
            s_ref[slot, g] = t
            maxima += (jnp.max(t, axis=0, keepdims=True),)
        return maxima

    def fold(j, slot, maxima, stats):
        out = ()
        for g in range(G):
            shift = -slopes[g] * (q0 - j * tk).astype(F32)
            out += _softmax_fold(s_ref[slot, g], maxima[g], shift, vt_ref[g, j], stats[2 * g], stats[2 * g + 1],
                                 acc_ref, g)
        return out

    stats = (jnp.full((1, BLK), NEG, F32), jnp.zeros((1, BLK), F32)) * G
    stats = _pipelined_tiles(n_last, logits, fold, stats)
    for g in range(G):
        o_ref[:, cols[g]] = (acc_ref[g] * (1.0 / stats[2 * g + 1])).T.astype(o_ref.dtype)


def _moba_attention(zb, slopes, batch, tk=ATTN_TK, G=MOBA_HEADS_PER_STEP):
    T = zb.shape[0]
    S = T // batch
    H = MOBA_HEADS
    BLK = MOBA_BLOCK
    assert S % tk == 0 and tk % BLK == 0 and H % G == 0
    nb = S // BLK
    nk = S // tk
    nbp = -(-nb // 16) * 16
    W = G * HEAD_DIM
    vt = zb[:, 2 * H * HEAD_DIM:].reshape(batch, nk, tk, H, HEAD_DIM).transpose(0, 3, 1, 4, 2)
    smem = pl.BlockSpec(memory_space=pltpu.SMEM)
    return pl.pallas_call(
        functools.partial(_moba_body, nb=nb, tk=tk, G=G),
        grid=(batch, H // G, nb),
        in_specs=[smem,
                  pl.BlockSpec((BLK, W), lambda b, h, i: (b * nb + i, h)),
                  pl.BlockSpec((S, W), lambda b, h, i: (b, H // G + h)),
                  pl.BlockSpec((None, G, nk, HEAD_DIM, tk), lambda b, h, i: (b, h, 0, 0, 0))],
        out_specs=pl.BlockSpec((BLK, W), lambda b, h, i: (b * nb + i, h)),
        out_shape=jax.ShapeDtypeStruct((T, H * HEAD_DIM), BF16),
        scratch_shapes=[pltpu.VMEM((G, nbp, HEAD_DIM), F32), pltpu.VMEM((G, nbp, BLK), F32),
                        pltpu.VMEM((G, HEAD_DIM, BLK), F32), pltpu.VMEM((2, G, tk, BLK), F32),
                        pltpu.VMEM((G, 1 + tk // BLK, tk, BLK), F32)],
        compiler_params=_params("parallel", "parallel", "arbitrary"),
        name="moba_attention",
    )(slopes, zb, zb, vt)


def _log_sigmoid(x):
    return jnp.minimum(x, 0.0) - jnp.log(1.0 + jnp.exp(-jnp.abs(x)))


def _split3(x):
    x1 = x.astype(BF16)
    r1 = x - x1.astype(F32)
    x2 = r1.astype(BF16)
    x3 = (r1 - x2.astype(F32)).astype(BF16)
    return x1, x2, x3


def _mlstm_body(qk_ref, v_ref, og_ref, ifc_ref, ifr_ref, cw_ref, cb_ref, gbc_ref, gbr_ref, ng_ref, out_ref,
                xext_ref, ct_ref, n_ref, m_ref, *, L):
    H, DK, DV = ML_HEADS, HEAD_DIM, ML_V
    PAD = SUBLANES
    c = pl.program_id(1)

    @pl.when(c == 0)
    def _():
        xext_ref[0:PAD, :] = jnp.zeros((PAD, xext_ref.shape[1]), F32)
        ct_ref[...] = jnp.zeros(ct_ref.shape, F32)
        n_ref[...] = jnp.zeros(n_ref.shape, F32)
        m_ref[...] = jnp.zeros(m_ref.shape, F32)

    xext_ref[PAD:PAD + L, :] = qk_ref[...].astype(F32)

    def conv_silu(col0):
        cs = slice(col0, col0 + DK)
        y = cb_ref[:, cs] + xext_ref[PAD:PAD + L, cs] * cw_ref[CONV_W - 1:CONV_W, cs]
        for j in range(CONV_W - 1):
            off = PAD - (CONV_W - 1) + j
            y = y + xext_ref[off:off + L, cs] * cw_ref[j:j + 1, cs]
        return y * jax.nn.sigmoid(y)

    rt = lax.broadcasted_iota(jnp.int32, (L, L), 0)
    cl = lax.broadcasted_iota(jnp.int32, (L, L), 1)
    tril = rt >= cl
    ones_tril = jnp.where(tril, 1.0, 0.0).astype(BF16)
    strict = rt > cl

    for h in range(H):
        qh = conv_silu(h * DK)
        kh = conv_silu(H * DK + h * DK) * (DK ** -0.5)
        qb = qh.astype(BF16)
        kb = kh.astype(BF16)
        vh = v_ref[:, h * DV:(h + 1) * DV]
        i_col = ifc_ref[:, h:h + 1] + gbc_ref[:, h:h + 1]
        lf_col = _log_sigmoid(ifc_ref[:, H + h:H + h + 1] + gbc_ref[:, H + h:H + h + 1])
        i_row = ifr_ref[h:h + 1, :] + gbr_ref[h:h + 1, :]
        b1, b2, b3 = _split3(jnp.where(strict, lf_col, 0.0))
        dp = (jnp.dot(ones_tril, b1, preferred_element_type=F32)
              + jnp.dot(ones_tril, b2, preferred_element_type=F32)
              + jnp.dot(ones_tril, b3, preferred_element_type=F32))
        g_col = dp[:, 0:1] + lf_col[0:1, :]
        g_last = g_col[L - 1:L, :]
        m_prev = m_ref[h][:, 0:1]
        d = jnp.where(tril, dp + i_row, NEG)
        inter = g_col + m_prev
        m_t = jnp.maximum(inter, jnp.max(d, axis=-1, keepdims=True))
        w = jnp.exp(d - m_t)
        a = jnp.exp(inter - m_t)
        sw = lax.dot_general(qb, kb, NT_DIMS, preferred_element_type=F32) * w
        num = (a * jnp.dot(qb, ct_ref[h].astype(BF16), preferred_element_type=F32)
               + jnp.dot(sw.astype(BF16), vh, preferred_element_type=F32))
        den = a * jnp.sum(qh * n_ref[h], axis=-1, keepdims=True) + jnp.sum(sw, axis=-1, keepdims=True)
        hh = num / jnp.maximum(jnp.abs(den), jnp.exp(-m_t))
        a_last = g_last - g_col + i_col
        m_new = jnp.maximum(g_last + m_prev, jnp.max(a_last, axis=0, keepdims=True))
        kw = kh * jnp.exp(a_last - m_new)
        decay = jnp.exp(g_last + m_prev - m_new)
        ct_ref[h] = decay * ct_ref[h] + lax.dot_general(kw.astype(BF16), vh, TN_DIMS, preferred_element_type=F32)
        n_ref[h] = decay * n_ref[h] + jnp.sum(kw, axis=0, keepdims=True)
        m_ref[h] = jnp.broadcast_to(m_new, (1, LANES))
        mu = jnp.mean(hh, axis=-1, keepdims=True)
        xc = hh - mu
        var = jnp.mean(xc * xc, axis=-1, keepdims=True)
        vs = slice(h * DV, (h + 1) * DV)
        yn = xc * lax.rsqrt(var + EPS) * ng_ref[:, vs]
        out_ref[:, vs] = (yn * jax.nn.sigmoid(og_ref[:, vs].astype(F32))).astype(out_ref.dtype)

    xext_ref[0:PAD, :] = xext_ref[L:L + PAD, :]


def _mlstm(zm, zif, conv_w, conv_b, gate_b, norm_g, batch, L=256):
    T = zm.shape[0]
    S = T // batch
    H = ML_HEADS
    L = min(L, S)
    nc = S // L
    W = 2 * H * HEAD_DIM
    zif_row = zif.reshape(batch, S, 2 * H).transpose(0, 2, 1)
    gb = gate_b.astype(F32).reshape(2 * H)
    row = lambda b, c: (b * nc + c, 0)
    return pl.pallas_call(
        functools.partial(_mlstm_body, L=L),
        grid=(batch, nc),
        in_specs=[pl.BlockSpec((L, W), row),
                  pl.BlockSpec((L, W), lambda b, c: (b * nc + c, 1)),
                  pl.BlockSpec((L, W), lambda b, c: (b * nc + c, 2)),
                  pl.BlockSpec((L, 2 * H), row),
                  pl.BlockSpec((None, 2 * H, L), lambda b, c: (b, 0, c)),
                  pl.BlockSpec((CONV_W, W), lambda b, c: (0, 0)),
                  pl.BlockSpec((1, W), lambda b, c: (0, 0)),
                  pl.BlockSpec((1, 2 * H), lambda b, c: (0, 0)),
                  pl.BlockSpec((2 * H, 1), lambda b, c: (0, 0)),
                  pl.BlockSpec((1, W), lambda b, c: (0, 0))],
        out_specs=pl.BlockSpec((L, W), row),
        out_shape=jax.ShapeDtypeStruct((T, W), BF16),
        scratch_shapes=[pltpu.VMEM((L + 2 * SUBLANES, W), F32),
                        pltpu.VMEM((H, HEAD_DIM, ML_V), F32),
                        pltpu.VMEM((H, 1, HEAD_DIM), F32),
                        pltpu.VMEM((H, 1, LANES), F32)],
        compiler_params=_params("parallel", "arbitrary"),
        name="mlstm",
    )(zm, zm, zm, zif, zif_row, conv_w.astype(F32), conv_b.reshape(1, W).astype(F32),
      gb.reshape(1, 2 * H), gb.reshape(2 * H, 1), norm_g.reshape(1, W).astype(F32))


def _alibi_slopes(n):
    return jnp.asarray(2.0 ** (-8.0 * jnp.arange(1, n + 1, dtype=F32) / n), dtype=F32)


def _mixer(xf, batch, layer_idx, norm1_g, w_in, diff_lambda, diff_norm_g, ml_conv_w, ml_conv_b, ml_gate_b,
           ml_norm_g, w_branch, w_out):
    D = xf.shape[1]
    BW = DIFF_HEADS * DIFF_V
    h = _rmsnorm(xf, norm1_g, BF16)
    n_qkv = 3 * BW
    c0, c1 = n_qkv, 2 * n_qkv
    c2 = c1 + 2 * ML_HEADS
    c3 = c2 + n_qkv
    zd = _matmul(h, w_in[:, :c0].astype(BF16), BF16, 1024, 512, name="proj_diff")
    zm = _matmul(h, w_in[:, c0:c1].astype(BF16), BF16, 1024, 512, name="proj_mlstm")
    w_if = jnp.pad(w_in[:, c1:c2], ((0, 0), (0, LANES - 2 * ML_HEADS))).astype(BF16)
    zif = _matmul(h, w_if, F32, 1024, LANES, name="proj_if")[:, :2 * ML_HEADS]
    zb = _matmul(h, w_in[:, c2:c3].astype(BF16), BF16, 1024, 512, name="proj_moba")
    zg = _matmul(h, w_in[:, c3:].astype(BF16), BF16, 1024, 512, name="proj_gates")

    lam_init = 0.8 - 0.6 * math.exp(-0.3 * layer_idx)
    ya = _diff_attention(zd, diff_lambda, diff_norm_g, _alibi_slopes(DIFF_HEADS), batch, lam_init)
    yb = _mlstm(zm, zif, ml_conv_w, ml_conv_b, ml_gate_b, ml_norm_g, batch)
    yc = _moba_attention(zb, _alibi_slopes(MOBA_HEADS), batch)
    merged = _merge(ya, yb, yc, w_branch.astype(BF16), zg)
    return _matmul(merged, w_out.astype(BF16), F32, 1024, 512, residual=xf, name="proj_out")


def _ffn(xf, norm2_g, w_gate_up, w_down):
    h = _rmsnorm(xf, norm2_g, BF16)
    act = _swiglu_up(h, w_gate_up.astype(BF16))
    return _matmul(act, w_down.astype(BF16), F32, 512, 256, residual=xf, name="ffn_down")


def kernel(x, norm1_g, w_in, diff_lambda, diff_norm_g, ml_conv_w, ml_conv_b, ml_gate_b, ml_norm_g, w_branch, w_out,
           norm2_g, w_gate_up, w_down, final_g):
    B, S, D = x.shape
    xf = x.reshape(B * S, D)
    for l in range(w_in.shape[0]):
        xf = _mixer(xf, B, l, norm1_g[l], w_in[l], diff_lambda[l], diff_norm_g[l], ml_conv_w[l], ml_conv_b[l],
                    ml_gate_b[l], ml_norm_g[l], w_branch[l], w_out[l])
        xf = _ffn(xf, norm2_g[l], w_gate_up[l], w_down[l])
    return _rmsnorm(xf, final_g, F32).reshape(B, S, D)
```

```python
import functools
import math

import jax
import jax.numpy as jnp
from jax import lax
from jax.experimental import pallas as pl
from jax.experimental.pallas import tpu as pltpu

F32 = jnp.float32
BF16 = jnp.bfloat16

HEAD_DIM = 128
DIFF_HEADS = 8
DIFF_V = 2 * HEAD_DIM
ML_HEADS = 8
ML_V = 2 * HEAD_DIM
CONV_W = 4
MOBA_HEADS = 16
MOBA_BLOCK = 256
MOBA_TOPK = 3
N_BRANCH = 3
EPS = 1e-6
NEG = -1e30
LOG2E = math.log2(math.e)

LANES = 128
SUBLANES = 8
VMEM_LIMIT = 48 * 1024 * 1024

ATTN_TQ = 512
ATTN_TK = 512
MOBA_HEADS_PER_STEP = 2

NT_DIMS = (((1,), (1,)), ((), ()))
TN_DIMS = (((0,), (0,)), ((), ()))


def _params(*sem):
    return pltpu.CompilerParams(dimension_semantics=sem, vmem_limit_bytes=VMEM_LIMIT)


def _rmsnorm_body(x_ref, g_ref, o_ref):
    x = x_ref[...]
    ms = jnp.mean(x * x, axis=-1, keepdims=True)
    o_ref[...] = (x * lax.rsqrt(ms + EPS) * g_ref[...]).astype(o_ref.dtype)


def _rmsnorm(x, g, out_dtype, tr=256):
    T, D = x.shape
    tr = min(tr, T)
    return pl.pallas_call(
        _rmsnorm_body,
        grid=(T // tr,),
        in_specs=[pl.BlockSpec((tr, D), lambda i: (i, 0)), pl.BlockSpec((1, D), lambda i: (0, 0))],
        out_specs=pl.BlockSpec((tr, D), lambda i: (i, 0)),
        out_shape=jax.ShapeDtypeStruct((T, D), out_dtype),
        compiler_params=_params("parallel"),
        name="rmsnorm",
    )(x, g.reshape(1, D).astype(F32))


def _mm_body(a_ref, b_ref, o_ref):
    o_ref[...] = jnp.dot(a_ref[...], b_ref[...], preferred_element_type=F32).astype(o_ref.dtype)


def _mm_res_body(a_ref, b_ref, r_ref, o_ref):
    o_ref[...] = (r_ref[...] + jnp.dot(a_ref[...], b_ref[...], preferred_element_type=F32)).astype(o_ref.dtype)


def _matmul(a, b, out_dtype, tm, tn, residual=None, name="matmul"):
    M, K = a.shape
    N = b.shape[1]
    tm, tn = min(tm, M), min(tn, N)
    assert M % tm == 0 and N % tn == 0
    in_specs = [pl.BlockSpec((tm, K), lambda i, j: (i, 0)), pl.BlockSpec((K, tn), lambda i, j: (0, j))]
    args = [a, b]
    body = _mm_body
    if residual is not None:
        in_specs.append(pl.BlockSpec((tm, tn), lambda i, j: (i, j)))
        args.append(residual)
        body = _mm_res_body
    return pl.pallas_call(
        body,
        grid=(M // tm, N // tn),
        in_specs=in_specs,
        out_specs=pl.BlockSpec((tm, tn), lambda i, j: (i, j)),
        out_shape=jax.ShapeDtypeStruct((M, N), out_dtype),
        compiler_params=_params("parallel", "parallel"),
        name=name,
    )(*args)


def _mm_t_body(a_ref, b_ref, o_ref):
    o_ref[...] = jnp.dot(a_ref[...], b_ref[...], preferred_element_type=F32).T.astype(o_ref.dtype)


def _matmul_t(a, b, out_dtype, tm, tn, name="matmul_t"):
    M, K = a.shape
    N = b.shape[1]
    tm, tn = min(tm, M), min(tn, N)
    assert M % tm == 0 and N % tn == 0
    return pl.pallas_call(
        _mm_t_body,
        grid=(M // tm, N // tn),
        in_specs=[pl.BlockSpec((tm, K), lambda i, j: (i, 0)), pl.BlockSpec((K, tn), lambda i, j: (0, j))],
        out_specs=pl.BlockSpec((tn, tm), lambda i, j: (j, i)),
        out_shape=jax.ShapeDtypeStruct((N, M), out_dtype),
        compiler_params=_params("parallel", "parallel"),
        name=name,
    )(a, b)


def _glu_body(a_ref, wg_ref, wu_ref, o_ref):
    a = a_ref[...]
    g = jnp.dot(a, wg_ref[...], preferred_element_type=F32)
    u = jnp.dot(a, wu_ref[...], preferred_element_type=F32)
    o_ref[...] = (g * jax.nn.sigmoid(g) * u).astype(o_ref.dtype)


def _swiglu_up(a, w_gu, tm=1024, tn=256):
    M, K = a.shape
    F = w_gu.shape[1] // 2
    tm = min(tm, M)
    nj = F // tn
    assert F % tn == 0 and M % tm == 0
    return pl.pallas_call(
        _glu_body,
        grid=(M // tm, nj),
        in_specs=[pl.BlockSpec((tm, K), lambda i, j: (i, 0)),
                  pl.BlockSpec((K, tn), lambda i, j: (0, j)),
                  pl.BlockSpec((K, tn), lambda i, j: (0, nj + j))],
        out_specs=pl.BlockSpec((tm, tn), lambda i, j: (i, j)),
        out_shape=jax.ShapeDtypeStruct((M, F), BF16),
        compiler_params=_params("parallel", "parallel"),
        name="swiglu_up",
    )(a, w_gu, w_gu)


def _merge_body(ya_ref, yb_ref, yc_ref, w0_ref, w1_ref, w2_ref, g0_ref, g1_ref, g2_ref, o_ref):
    acc = jax.nn.sigmoid(g0_ref[...].astype(F32)) * jnp.dot(ya_ref[...], w0_ref[...], preferred_element_type=F32)
    acc += jax.nn.sigmoid(g1_ref[...].astype(F32)) * jnp.dot(yb_ref[...], w1_ref[...], preferred_element_type=F32)
    acc += jax.nn.sigmoid(g2_ref[...].astype(F32)) * jnp.dot(yc_ref[...], w2_ref[...], preferred_element_type=F32)
    o_ref[...] = acc.astype(o_ref.dtype)


def _merge(ya, yb, yc, w_branch, gates, tm=512, tn=512):
    M, K = ya.shape
    D = w_branch.shape[2]
    tm, tn = min(tm, M), min(tn, D)
    nj = D // tn
    y_spec = pl.BlockSpec((tm, K), lambda i, j: (i, 0))
    w_specs = [pl.BlockSpec((None, K, tn), functools.partial(lambda i, j, b: (b, 0, j), b=b)) for b in range(N_BRANCH)]
    g_specs = [pl.BlockSpec((tm, tn), functools.partial(lambda i, j, b: (i, b * nj + j), b=b)) for b in range(N_BRANCH)]
    return pl.pallas_call(
        _merge_body,
        grid=(M // tm, nj),
        in_specs=[y_spec, y_spec, y_spec] + w_specs + g_specs,
        out_specs=pl.BlockSpec((tm, tn), lambda i, j: (i, j)),
        out_shape=jax.ShapeDtypeStruct((M, D), BF16),
        compiler_params=_params("parallel", "parallel"),
        name="merge",
    )(ya, yb, yc, w_branch, w_branch, w_branch, gates, gates, gates)


def _softmax_fold(t, tmax, shift, vt, m_prev, l_prev, acc_ref, idx):
    m_new = jnp.maximum(m_prev, tmax + shift)
    p = jnp.exp2(t - (m_new - shift))
    alpha = jnp.exp2(m_prev - m_new)
    l_new = alpha * l_prev + jnp.sum(p, axis=0, keepdims=True)
    acc_ref[idx] = alpha * acc_ref[idx] + jnp.dot(vt, p.astype(BF16), preferred_element_type=F32)
    return m_new, l_new


def _pipelined_tiles(n_last, logits_fn, fold_fn, stats):
    def piped(j, carry, slot):
        stats, maxima = carry
        nxt = logits_fn(j + 1, 1 - slot)
        return fold_fn(j, slot, maxima, stats), nxt

    def body(j, carry):
        return lax.cond((j & 1) == 0, lambda c: piped(j, c, 0), lambda c: piped(j, c, 1), carry)

    carry = lax.fori_loop(0, n_last, body, (stats, logits_fn(0, 0)))
    return lax.cond((n_last & 1) == 0,
                    lambda c: fold_fn(n_last, 0, c[1], c[0]),
                    lambda c: fold_fn(n_last, 1, c[1], c[0]), carry)


def _alibi_bias_tiles(bias_ref, tk, tq, slope):
    off = lax.broadcasted_iota(jnp.int32, (tk, tq), 0) - lax.broadcasted_iota(jnp.int32, (tk, tq), 1)
    rel = off.astype(F32) * slope
    bias_ref[0] = rel
    for r in range(tk // tq):
        bias_ref[1 + r] = jnp.where(off <= r * tq, rel, NEG)


def _diff_body(slopes_ref, lam_ref, q_ref, k_ref, vt_ref, g_ref, o_ref, acc_ref, s_ref, bias_ref, *, tq, tk,
               lam_init):
    h = pl.program_id(1)
    qi = pl.program_id(2)
    slope = slopes_ref[h] * LOG2E
    scale = HEAD_DIM ** -0.5 * LOG2E
    lp = lam_ref[...]
    lam = (jnp.exp(jnp.sum(lp[0:1] * lp[1:2], axis=-1, keepdims=True))
           - jnp.exp(jnp.sum(lp[2:3] * lp[3:4], axis=-1, keepdims=True)) + lam_init)
    q0 = qi * tq
    n_last = q0 // tk
    last_bias = 1 + (q0 - n_last * tk) // tq

    @pl.when(qi == 0)
    def _():
        _alibi_bias_tiles(bias_ref, tk, tq, slope)

    acc_ref[...] = jnp.zeros(acc_ref.shape, F32)

    def logits(j, slot):
        k0 = pl.multiple_of(j * tk, tk)
        bias = bias_ref[jnp.where(j == n_last, last_bias, 0)]
        maxima = ()
        for c in range(2):
            q = q_ref[:, c * HEAD_DIM:(c + 1) * HEAD_DIM]
            k = k_ref[pl.ds(k0, tk), c * HEAD_DIM:(c + 1) * HEAD_DIM]
            t = lax.dot_general(k, q, NT_DIMS, preferred_element_type=F32) * scale + bias
            s_ref[slot, c] = t
            maxima += (jnp.max(t, axis=0, keepdims=True),)
        return maxima

    def fold(j, slot, maxima, stats):
        shift = -slope * (q0 - j * tk).astype(F32)
        vt = vt_ref[:, pl.ds(pl.multiple_of(j * tk, tk), tk)]
        out = ()
        for c in range(2):
            out += _softmax_fold(s_ref[slot, c], maxima[c], shift, vt, stats[2 * c], stats[2 * c + 1], acc_ref, c)
        return out

    m_init = jnp.full((1, tq), NEG, F32)
    l_init = jnp.zeros((1, tq), F32)
    _, l0, _, l1 = _pipelined_tiles(n_last, logits, fold, (m_init, l_init, m_init, l_init))

    ot = acc_ref[0] * (1.0 / l0) - acc_ref[1] * (lam * (1.0 / l1))
    o = ot.T
    ms = jnp.mean(o * o, axis=-1, keepdims=True)
    y = (o * lax.rsqrt(ms + EPS) * g_ref[...]) * (1.0 - lam_init)
    o_ref[...] = y.astype(o_ref.dtype)


def _diff_attention(zqk, vt, diff_lambda, norm_g, slopes, batch, lam_init, tq=ATTN_TQ, tk=ATTN_TK):
    T = zqk.shape[0]
    S = T // batch
    H = DIFF_HEADS
    nq = S // tq
    assert S % tk == 0 and tk % tq == 0
    W = 2 * HEAD_DIM
    smem = pl.BlockSpec(memory_space=pltpu.SMEM)
    return pl.pallas_call(
        functools.partial(_diff_body, tq=tq, tk=tk, lam_init=lam_init),
        grid=(batch, H, nq),
        in_specs=[smem,
                  pl.BlockSpec((4, HEAD_DIM), lambda b, h, i: (0, 0)),
                  pl.BlockSpec((tq, W), lambda b, h, i: (b * nq + i, h)),
                  pl.BlockSpec((S, W), lambda b, h, i: (b, H + h)),
                  pl.BlockSpec((W, S), lambda b, h, i: (h, b)),
                  pl.BlockSpec((1, W), lambda b, h, i: (0, 0))],
        out_specs=pl.BlockSpec((tq, W), lambda b, h, i: (b * nq + i, h)),
        out_shape=jax.ShapeDtypeStruct((T, H * W), BF16),
        scratch_shapes=[pltpu.VMEM((2, W, tq), F32), pltpu.VMEM((2, 2, tk, tq), F32),
                        pltpu.VMEM((1 + tk // tq, tk, tq), F32)],
        compiler_params=_params("parallel", "parallel", "arbitrary"),
        name="diff_attention",
    )(slopes, diff_lambda.astype(F32), zqk, zqk, vt, norm_g.reshape(1, W).astype(F32))


def _moba_allowed(q, km, own, nb):
    nbp, nq = km.shape[0], q.shape[0]
    km_hi = km.astype(BF16)
    km_lo = (km - km_hi.astype(F32)).astype(BF16)
    gs = (lax.dot_general(km_hi, q, NT_DIMS, preferred_element_type=F32)
          + lax.dot_general(km_lo, q, NT_DIMS, preferred_element_type=F32))
    blk = lax.broadcasted_iota(jnp.int32, (nbp, nq), 0)
    rank = jnp.zeros((nbp, nq), F32)
    for m in range(nb):
        row = gs[m:m + 1, :]
        tie = jnp.where(blk > m, 1.0, 0.0)
        beats = jnp.where(row > gs, 1.0, jnp.where(row == gs, tie, 0.0))
        rank = rank + jnp.where(own > m, beats, 0.0)
    chosen = jnp.where(rank < float(min(MOBA_TOPK, nb)), 1.0, 0.0)
    return jnp.where(blk < own, chosen, jnp.where(blk == own, 1.0, 0.0))


def _moba_body(slopes_ref, q_ref, k_ref, vt_ref, o_ref, km_ref, ch_ref, acc_ref, s_ref, bias_ref, *, nb, tk, G):
    hp = pl.program_id(1)
    qi = pl.program_id(2)
    BLK = MOBA_BLOCK
    tq = q_ref.shape[0]
    bpt = tk // BLK
    scale = HEAD_DIM ** -0.5 * LOG2E
    slopes = [slopes_ref[hp * G + g] * LOG2E for g in range(G)]
    cols = [slice(g * HEAD_DIM, (g + 1) * HEAD_DIM) for g in range(G)]
    q0 = qi * tq
    n_last = q0 // tk
    last_bias = 1 + (q0 - n_last * tk) // tq
    own = (q0 + lax.broadcasted_iota(jnp.int32, (1, tq), 1)) // BLK

    @pl.when(qi == 0)
    def _():
        km_ref[...] = jnp.zeros(km_ref.shape, F32)
        for g in range(G):
            _alibi_bias_tiles(bias_ref.at[g], tk, tq, slopes[g])
            for n in range(nb):
                kb = k_ref[n * BLK:(n + 1) * BLK, cols[g]].astype(F32)
                km_ref[g, n:n + 1, :] = jnp.mean(kb, axis=0, keepdims=True)

    for g in range(G):
        ch_ref[g] = _moba_allowed(q_ref[:, cols[g]], km_ref[g], own, nb)
    acc_ref[...] = jnp.zeros(acc_ref.shape, F32)

    def logits(j, slot):
        k0 = pl.multiple_of(j * tk, tk)
        bidx = jnp.where(j == n_last, last_bias, 0)
        maxima = ()
        for g in range(G):
            t = lax.dot_general(k_ref[pl.ds(k0, tk), cols[g]], q_ref[:, cols[g]], NT_DIMS,
                                preferred_element_type=F32) * scale + bias_ref[g, bidx]
            parts = []
            for i in range(bpt):
                allowed = ch_ref[g, pl.ds(j * bpt + i, 1), :]
                parts.append(jnp.where(allowed > 0.5, t[i * BLK:(i + 1) * BLK], NEG))
            t = jnp.concatenate(parts, axis=0)
            s_ref[slot, g] = t
            maxima += (jnp.max(t, axis=0, keepdims=True),)
        return maxima

    def fold(j, slot, maxima, stats):
        out = ()
        for g in range(G):
            shift = -slopes[g] * (q0 - j * tk).astype(F32)
            vt = vt_ref[cols[g], pl.ds(pl.multiple_of(j * tk, tk), tk)]
            out += _softmax_fold(s_ref[slot, g], maxima[g], shift, vt, stats[2 * g], stats[2 * g + 1], acc_ref, g)
        return out

    stats = (jnp.full((1, tq), NEG, F32), jnp.zeros((1, tq), F32)) * G
    stats = _pipelined_tiles(n_last, logits, fold, stats)
    for g in range(G):
        o_ref[:, cols[g]] = (acc_ref[g] * (1.0 / stats[2 * g + 1])).T.astype(o_ref.dtype)


def _moba_attention(zqk, vt, slopes, batch, tq=ATTN_TQ, tk=ATTN_TK, G=MOBA_HEADS_PER_STEP):
    T = zqk.shape[0]
    S = T // batch
    H = MOBA_HEADS
    BLK = MOBA_BLOCK
    assert S % tk == 0 and tk % tq == 0 and tq % BLK == 0 and H % G == 0
    nb = S // BLK
    nq = S // tq
    nbp = -(-nb // 16) * 16
    W = G * HEAD_DIM
    smem = pl.BlockSpec(memory_space=pltpu.SMEM)
    return pl.pallas_call(
        functools.partial(_moba_body, nb=nb, tk=tk, G=G),
        grid=(batch, H // G, nq),
        in_specs=[smem,
                  pl.BlockSpec((tq, W), lambda b, h, i: (b * nq + i, h)),
                  pl.BlockSpec((S, W), lambda b, h, i: (b, H // G + h)),
                  pl.BlockSpec((W, S), lambda b, h, i: (h, b))],
        out_specs=pl.BlockSpec((tq, W), lambda b, h, i: (b * nq + i, h)),
        out_shape=jax.ShapeDtypeStruct((T, H * HEAD_DIM), BF16),
        scratch_shapes=[pltpu.VMEM((G, nbp, HEAD_DIM), F32), pltpu.VMEM((G, nbp, tq), F32),
                        pltpu.VMEM((G, HEAD_DIM, tq), F32), pltpu.VMEM((2, G, tk, tq), F32),
                        pltpu.VMEM((G, 1 + tk // tq, tk, tq), F32)],
        compiler_params=_params("parallel", "parallel", "arbitrary"),
        name="moba_attention",
    )(slopes, zqk, zqk, vt)


def _log_sigmoid(x):
    return jnp.minimum(x, 0.0) - jnp.log(1.0 + jnp.exp(-jnp.abs(x)))


def _split3(x):
    x1 = x.astype(BF16)
    r1 = x - x1.astype(F32)
    x2 = r1.astype(BF16)
    x3 = (r1 - x2.astype(F32)).astype(BF16)
    return x1, x2, x3


def _mlstm_body(qk_ref, v_ref, og_ref, ifc_ref, ifr_ref, cw_ref, cb_ref, gbc_ref, gbr_ref, ng_ref, out_ref,
                xext_ref, ct_ref, n_ref, m_ref, *, L):
    H, DK, DV = ML_HEADS, HEAD_DIM, ML_V
    PAD = SUBLANES
    c = pl.program_id(1)

    @pl.when(c == 0)
    def _():
        xext_ref[0:PAD, :] = jnp.zeros((PAD, xext_ref.shape[1]), F32)
        ct_ref[...] = jnp.zeros(ct_ref.shape, F32)
        n_ref[...] = jnp.zeros(n_ref.shape, F32)
        m_ref[...] = jnp.zeros(m_ref.shape, F32)

    xext_ref[PAD:PAD + L, :] = qk_ref[...].astype(F32)

    def conv_silu(col0):
        cs = slice(col0, col0 + DK)
        y = cb_ref[:, cs] + xext_ref[PAD:PAD + L, cs] * cw_ref[CONV_W - 1:CONV_W, cs]
        for j in range(CONV_W - 1):
            off = PAD - (CONV_W - 1) + j
            y = y + xext_ref[off:off + L, cs] * cw_ref[j:j + 1, cs]
        return y * jax.nn.sigmoid(y)

    rt = lax.broadcasted_iota(jnp.int32, (L, L), 0)
    cl = lax.broadcasted_iota(jnp.int32, (L, L), 1)
    tril = rt >= cl
    ones_tril = jnp.where(tril, 1.0, 0.0).astype(BF16)
    strict = rt > cl

    for h in range(H):
        qh = conv_silu(h * DK)
        kh = conv_silu(H * DK + h * DK) * (DK ** -0.5)
        qb = qh.astype(BF16)
        kb = kh.astype(BF16)
        vh = v_ref[:, h * DV:(h + 1) * DV]
        i_col = ifc_ref[:, h:h + 1] + gbc_ref[:, h:h + 1]
        lf_col = _log_sigmoid(ifc_ref[:, H + h:H + h + 1] + gbc_ref[:, H + h:H + h + 1])
        i_row = ifr_ref[h:h + 1, :] + gbr_ref[h:h + 1, :]
        b1, b2, b3 = _split3(jnp.where(strict, lf_col, 0.0))
        dp = (jnp.dot(ones_tril, b1, preferred_element_type=F32)
              + jnp.dot(ones_tril, b2, preferred_element_type=F32)
              + jnp.dot(ones_tril, b3, preferred_element_type=F32))
        g_col = dp[:, 0:1] + lf_col[0:1, :]
        g_last = g_col[L - 1:L, :]
        m_prev = m_ref[h][:, 0:1]
        d = jnp.where(tril, dp + i_row, NEG)
        inter = g_col + m_prev
        m_t = jnp.maximum(inter, jnp.max(d, axis=-1, keepdims=True))
        w = jnp.exp(d - m_t)
        a = jnp.exp(inter - m_t)
        sw = lax.dot_general(qb, kb, NT_DIMS, preferred_element_type=F32) * w
        num = (a * jnp.dot(qb, ct_ref[h].astype(BF16), preferred_element_type=F32)
               + jnp.dot(sw.astype(BF16), vh, preferred_element_type=F32))
        den = a * jnp.sum(qh * n_ref[h], axis=-1, keepdims=True) + jnp.sum(sw, axis=-1, keepdims=True)
        hh = num / jnp.maximum(jnp.abs(den), jnp.exp(-m_t))
        a_last = g_last - g_col + i_col
        m_new = jnp.maximum(g_last + m_prev, jnp.max(a_last, axis=0, keepdims=True))
        kw = kh * jnp.exp(a_last - m_new)
        decay = jnp.exp(g_last + m_prev - m_new)
        ct_ref[h] = decay * ct_ref[h] + lax.dot_general(kw.astype(BF16), vh, TN_DIMS, preferred_element_type=F32)
        n_ref[h] = decay * n_ref[h] + jnp.sum(kw, axis=0, keepdims=True)
        m_ref[h] = jnp.broadcast_to(m_new, (1, LANES))
        mu = jnp.mean(hh, axis=-1, keepdims=True)
        xc = hh - mu
        var = jnp.mean(xc * xc, axis=-1, keepdims=True)
        vs = slice(h * DV, (h + 1) * DV)
        yn = xc * lax.rsqrt(var + EPS) * ng_ref[:, vs]
        out_ref[:, vs] = (yn * jax.nn.sigmoid(og_ref[:, vs].astype(F32))).astype(out_ref.dtype)

    xext_ref[0:PAD, :] = xext_ref[L:L + PAD, :]


def _mlstm(zm, zif, conv_w, conv_b, gate_b, norm_g, batch, L=256):
    T = zm.shape[0]
    S = T // batch
    H = ML_HEADS
    L = min(L, S)
    nc = S // L
    W = 2 * H * HEAD_DIM
    zif_row = zif.reshape(batch, S, 2 * H).transpose(0, 2, 1)
    gb = gate_b.astype(F32).reshape(2 * H)
    row = lambda b, c: (b * nc + c, 0)
    return pl.pallas_call(
        functools.partial(_mlstm_body, L=L),
        grid=(batch, nc),
        in_specs=[pl.BlockSpec((L, W), row),
                  pl.BlockSpec((L, W), lambda b, c: (b * nc + c, 1)),
                  pl.BlockSpec((L, W), lambda b, c: (b * nc + c, 2)),
                  pl.BlockSpec((L, 2 * H), row),
                  pl.BlockSpec((None, 2 * H, L), lambda b, c: (b, 0, c)),
                  pl.BlockSpec((CONV_W, W), lambda b, c: (0, 0)),
                  pl.BlockSpec((1, W), lambda b, c: (0, 0)),
                  pl.BlockSpec((1, 2 * H), lambda b, c: (0, 0)),
                  pl.BlockSpec((2 * H, 1), lambda b, c: (0, 0)),
                  pl.BlockSpec((1, W), lambda b, c: (0, 0))],
        out_specs=pl.BlockSpec((L, W), row),
        out_shape=jax.ShapeDtypeStruct((T, W), BF16),
        scratch_shapes=[pltpu.VMEM((L + 2 * SUBLANES, W), F32),
                        pltpu.VMEM((H, HEAD_DIM, ML_V), F32),
                        pltpu.VMEM((H, 1, HEAD_DIM), F32),
                        pltpu.VMEM((H, 1, LANES), F32)],
        compiler_params=_params("parallel", "arbitrary"),
        name="mlstm",
    )(zm, zm, zm, zif, zif_row, conv_w.astype(F32), conv_b.reshape(1, W).astype(F32),
      gb.reshape(1, 2 * H), gb.reshape(2 * H, 1), norm_g.reshape(1, W).astype(F32))


def _alibi_slopes(n):
    return jnp.asarray(2.0 ** (-8.0 * jnp.arange(1, n + 1, dtype=F32) / n), dtype=F32)


def _mixer(xf, batch, layer_idx, norm1_g, w_in, diff_lambda, diff_norm_g, ml_conv_w, ml_conv_b, ml_gate_b,
           ml_norm_g, w_branch, w_out):
    D = xf.shape[1]
    BW = DIFF_HEADS * DIFF_V
    h = _rmsnorm(xf, norm1_g, BF16)
    n_qkv = 3 * BW
    c0, c1 = n_qkv, 2 * n_qkv
    c2 = c1 + 2 * ML_HEADS
    c3 = c2 + n_qkv
    zd = _matmul(h, w_in[:, :2 * BW].astype(BF16), BF16, 1024, 512, name="proj_diff_qk")
    vd = _matmul_t(h, w_in[:, 2 * BW:c0].astype(BF16), BF16, 1024, 512, name="proj_diff_v")
    zm = _matmul(h, w_in[:, c0:c1].astype(BF16), BF16, 1024, 512, name="proj_mlstm")
    w_if = jnp.pad(w_in[:, c1:c2], ((0, 0), (0, LANES - 2 * ML_HEADS))).astype(BF16)
    zif = _matmul(h, w_if, F32, 1024, LANES, name="proj_if")[:, :2 * ML_HEADS]
    zb = _matmul(h, w_in[:, c2:c2 + 2 * BW].astype(BF16), BF16, 1024, 512, name="proj_moba_qk")
    vb = _matmul_t(h, w_in[:, c2 + 2 * BW:c3].astype(BF16), BF16, 1024, 512, name="proj_moba_v")
    zg = _matmul(h, w_in[:, c3:].astype(BF16), BF16, 1024, 512, name="proj_gates")

    lam_init = 0.8 - 0.6 * math.exp(-0.3 * layer_idx)
    ya = _diff_attention(zd, vd, diff_lambda, diff_norm_g, _alibi_slopes(DIFF_HEADS), batch, lam_init)
    yb = _mlstm(zm, zif, ml_conv_w, ml_conv_b, ml_gate_b, ml_norm_g, batch)
    yc = _moba_attention(zb, vb, _alibi_slopes(MOBA_HEADS), batch)
    merged = _merge(ya, yb, yc, w_branch.astype(BF16), zg)
    return _matmul(merged, w_out.astype(BF16), F32, 1024, 512, residual=xf, name="proj_out")


def _ffn(xf, norm2_g, w_gate_up, w_down):
    h = _rmsnorm(xf, norm2_g, BF16)
    act = _swiglu_up(h, w_gate_up.astype(BF16))
    return _matmul(act, w_down.astype(BF16), F32, 512, 256, residual=xf, name="ffn_down")


def kernel(x, norm1_g, w_in, diff_lambda, diff_norm_g, ml_conv_w, ml_conv_b, ml_gate_b, ml_norm_g, w_branch, w_out,
           norm2_g, w_gate_up, w_down, final_g):
    B, S, D = x.shape
    xf = x.reshape(B * S, D)
    for l in range(w_in.shape[0]):
        xf = _mixer(xf, B, l, norm1_g[l], w_in[l], diff_lambda[l], diff_norm_g[l], ml_conv_w[l], ml_conv_b[l],
                    ml_gate_b[l], ml_norm_g[l], w_branch[l], w_out[l])
        xf = _ffn(xf, norm2_g[l], w_gate_up[l], w_down[l])
    return _rmsnorm(xf, final_g, F32).reshape(B, S, D)
```

```python
import functools
import math

import jax
import jax.numpy as jnp
from jax import lax
from jax.experimental import pallas as pl
from jax.experimental.pallas import tpu as pltpu

F32 = jnp.float32
BF16 = jnp.bfloat16

HEAD_DIM = 128
DIFF_HEADS = 8
DIFF_V = 2 * HEAD_DIM
ML_HEADS = 8
ML_V = 2 * HEAD_DIM
CONV_W = 4
MOBA_HEADS = 16
MOBA_BLOCK = 256
MOBA_TOPK = 3
N_BRANCH = 3
EPS = 1e-6
NEG = -1e30
LOG2E = math.log2(math.e)

LANES = 128
SUBLANES = 8
VMEM_LIMIT = 48 * 1024 * 1024

PROJ_TM = 1024
PROJ_TN = 512
ATTN_TQ = 512
ATTN_TK = 512
MOBA_HEADS_PER_STEP = 2

NT_DIMS = (((1,), (1,)), ((), ()))
TN_DIMS = (((0,), (0,)), ((), ()))


def _params(*sem):
    return pltpu.CompilerParams(dimension_semantics=sem, vmem_limit_bytes=VMEM_LIMIT)


def _rmsnorm_body(x_ref, g_ref, o_ref):
    x = x_ref[...]
    ms = jnp.mean(x * x, axis=-1, keepdims=True)
    o_ref[...] = (x * lax.rsqrt(ms + EPS) * g_ref[...]).astype(o_ref.dtype)


def _rmsnorm(x, g, out_dtype, tr=256):
    T, D = x.shape
    tr = min(tr, T)
    return pl.pallas_call(
        _rmsnorm_body,
        grid=(T // tr,),
        in_specs=[pl.BlockSpec((tr, D), lambda i: (i, 0)), pl.BlockSpec((1, D), lambda i: (0, 0))],
        out_specs=pl.BlockSpec((tr, D), lambda i: (i, 0)),
        out_shape=jax.ShapeDtypeStruct((T, D), out_dtype),
        compiler_params=_params("parallel"),
        name="rmsnorm",
    )(x, g.reshape(1, D).astype(F32))


def _mm_body(a_ref, b_ref, o_ref):
    o_ref[...] = jnp.dot(a_ref[...], b_ref[...], preferred_element_type=F32).astype(o_ref.dtype)


def _mm_res_body(a_ref, b_ref, r_ref, o_ref):
    o_ref[...] = (r_ref[...] + jnp.dot(a_ref[...], b_ref[...], preferred_element_type=F32)).astype(o_ref.dtype)


def _weight_spec(w, l, cols, tn):
    c0, c1 = cols if cols is not None else (0, w.shape[2])
    assert c0 % tn == 0 and (c1 - c0) % tn == 0
    return pl.BlockSpec((None, w.shape[1], tn), lambda i, j: (l, 0, c0 // tn + j)), c1 - c0


def _matmul(a, w, l, out_dtype, tm, tn, residual=None, cols=None, name="matmul"):
    M, K = a.shape
    tm = min(tm, M)
    w_spec, N = _weight_spec(w, l, cols, tn)
    assert M % tm == 0
    in_specs = [pl.BlockSpec((tm, K), lambda i, j: (i, 0)), w_spec]
    args = [a, w]
    body = _mm_body
    if residual is not None:
        in_specs.append(pl.BlockSpec((tm, tn), lambda i, j: (i, j)))
        args.append(residual)
        body = _mm_res_body
    return pl.pallas_call(
        body,
        grid=(M // tm, N // tn),
        in_specs=in_specs,
        out_specs=pl.BlockSpec((tm, tn), lambda i, j: (i, j)),
        out_shape=jax.ShapeDtypeStruct((M, N), out_dtype),
        compiler_params=_params("parallel", "parallel"),
        name=name,
    )(*args)


def _mm_t_body(a_ref, b_ref, o_ref):
    o_ref[...] = jnp.dot(a_ref[...], b_ref[...], preferred_element_type=F32).T.astype(o_ref.dtype)


def _matmul_t(a, w, l, out_dtype, tm, tn, cols=None, name="matmul_t"):
    M, K = a.shape
    tm = min(tm, M)
    w_spec, N = _weight_spec(w, l, cols, tn)
    assert M % tm == 0
    return pl.pallas_call(
        _mm_t_body,
        grid=(M // tm, N // tn),
        in_specs=[pl.BlockSpec((tm, K), lambda i, j: (i, 0)), w_spec],
        out_specs=pl.BlockSpec((tn, tm), lambda i, j: (j, i)),
        out_shape=jax.ShapeDtypeStruct((N, M), out_dtype),
        compiler_params=_params("parallel", "parallel"),
        name=name,
    )(a, w)


def _glu_body(a_ref, wg_ref, wu_ref, o_ref):
    a = a_ref[...]
    g = jnp.dot(a, wg_ref[...], preferred_element_type=F32)
    u = jnp.dot(a, wu_ref[...], preferred_element_type=F32)
    o_ref[...] = (g * jax.nn.sigmoid(g) * u).astype(o_ref.dtype)


def _swiglu_up(a, w_gu, l, tm=1024, tn=256):
    M, K = a.shape
    F = w_gu.shape[2] // 2
    tm = min(tm, M)
    nj = F // tn
    assert F % tn == 0 and M % tm == 0
    return pl.pallas_call(
        _glu_body,
        grid=(M // tm, nj),
        in_specs=[pl.BlockSpec((tm, K), lambda i, j: (i, 0)),
                  pl.BlockSpec((None, K, tn), lambda i, j: (l, 0, j)),
                  pl.BlockSpec((None, K, tn), lambda i, j: (l, 0, nj + j))],
        out_specs=pl.BlockSpec((tm, tn), lambda i, j: (i, j)),
        out_shape=jax.ShapeDtypeStruct((M, F), BF16),
        compiler_params=_params("parallel", "parallel"),
        name="swiglu_up",
    )(a, w_gu, w_gu)


def _merge_body(ya_ref, yb_ref, yc_ref, w0_ref, w1_ref, w2_ref, g0_ref, g1_ref, g2_ref, o_ref):
    acc = jax.nn.sigmoid(g0_ref[...].astype(F32)) * jnp.dot(ya_ref[...], w0_ref[...], preferred_element_type=F32)
    acc += jax.nn.sigmoid(g1_ref[...].astype(F32)) * jnp.dot(yb_ref[...], w1_ref[...], preferred_element_type=F32)
    acc += jax.nn.sigmoid(g2_ref[...].astype(F32)) * jnp.dot(yc_ref[...], w2_ref[...], preferred_element_type=F32)
    o_ref[...] = acc.astype(o_ref.dtype)


def _merge(ya, yb, yc, w_branch, l, gates, tm=512, tn=512):
    M, K = ya.shape
    D = w_branch.shape[3]
    tm, tn = min(tm, M), min(tn, D)
    nj = D // tn
    y_spec = pl.BlockSpec((tm, K), lambda i, j: (i, 0))
    w_specs = [pl.BlockSpec((None, None, K, tn), functools.partial(lambda i, j, b: (l, b, 0, j), b=b))
               for b in range(N_BRANCH)]
    g_specs = [pl.BlockSpec((tm, tn), functools.partial(lambda i, j, b: (i, b * nj + j), b=b)) for b in range(N_BRANCH)]
    return pl.pallas_call(
        _merge_body,
        grid=(M // tm, nj),
        in_specs=[y_spec, y_spec, y_spec] + w_specs + g_specs,
        out_specs=pl.BlockSpec((tm, tn), lambda i, j: (i, j)),
        out_shape=jax.ShapeDtypeStruct((M, D), BF16),
        compiler_params=_params("parallel", "parallel"),
        name="merge",
    )(ya, yb, yc, w_branch, w_branch, w_branch, gates, gates, gates)


def _softmax_fold(t, tmax, shift, vt, m_prev, l_prev, acc_ref, idx):
    m_new = jnp.maximum(m_prev, tmax + shift)
    p = jnp.exp2(t - (m_new - shift))
    alpha = jnp.exp2(m_prev - m_new)
    l_new = alpha * l_prev + jnp.sum(p, axis=0, keepdims=True)
    acc_ref[idx] = alpha * acc_ref[idx] + jnp.dot(vt, p.astype(BF16), preferred_element_type=F32)
    return m_new, l_new


def _pipelined_tiles(n_last, logits_fn, fold_fn, stats):
    def piped(j, carry, slot):
        stats, maxima = carry
        nxt = logits_fn(j + 1, 1 - slot)
        return fold_fn(j, slot, maxima, stats), nxt

    def body(j, carry):
        return lax.cond((j & 1) == 0, lambda c: piped(j, c, 0), lambda c: piped(j, c, 1), carry)

    carry = lax.fori_loop(0, n_last, body, (stats, logits_fn(0, 0)))
    return lax.cond((n_last & 1) == 0,
                    lambda c: fold_fn(n_last, 0, c[1], c[0]),
                    lambda c: fold_fn(n_last, 1, c[1], c[0]), carry)


def _alibi_bias_tiles(bias_ref, tk, tq, slope):
    off = lax.broadcasted_iota(jnp.int32, (tk, tq), 0) - lax.broadcasted_iota(jnp.int32, (tk, tq), 1)
    rel = off.astype(F32) * slope
    bias_ref[0] = rel
    for r in range(tk // tq):
        bias_ref[1 + r] = jnp.where(off <= r * tq, rel, NEG)


def _diff_body(slopes_ref, lam_ref, q_ref, k_ref, vt_ref, g_ref, o_ref, acc_ref, s_ref, bias_ref, *, tq, tk,
               lam_init):
    h = pl.program_id(1)
    qi = pl.program_id(2)
    slope = slopes_ref[h] * LOG2E
    scale = HEAD_DIM ** -0.5 * LOG2E
    lp = lam_ref[...]
    lam = (jnp.exp(jnp.sum(lp[0:1] * lp[1:2], axis=-1, keepdims=True))
           - jnp.exp(jnp.sum(lp[2:3] * lp[3:4], axis=-1, keepdims=True)) + lam_init)
    q0 = qi * tq
    n_last = q0 // tk
    last_bias = 1 + (q0 - n_last * tk) // tq

    @pl.when(qi == 0)
    def _():
        _alibi_bias_tiles(bias_ref, tk, tq, slope)

    acc_ref[...] = jnp.zeros(acc_ref.shape, F32)

    def logits(j, slot):
        k0 = pl.multiple_of(j * tk, tk)
        bias = bias_ref[jnp.where(j == n_last, last_bias, 0)]
        maxima = ()
        for c in range(2):
            q = q_ref[:, c * HEAD_DIM:(c + 1) * HEAD_DIM]
            k = k_ref[pl.ds(k0, tk), c * HEAD_DIM:(c + 1) * HEAD_DIM]
            t = lax.dot_general(k, q, NT_DIMS, preferred_element_type=F32) * scale + bias
            s_ref[slot, c] = t
            maxima += (jnp.max(t, axis=0, keepdims=True),)
        return maxima

    def fold(j, slot, maxima, stats):
        shift = -slope * (q0 - j * tk).astype(F32)
        vt = vt_ref[:, pl.ds(pl.multiple_of(j * tk, tk), tk)]
        out = ()
        for c in range(2):
            out += _softmax_fold(s_ref[slot, c], maxima[c], shift, vt, stats[2 * c], stats[2 * c + 1], acc_ref, c)
        return out

    m_init = jnp.full((1, tq), NEG, F32)
    l_init = jnp.zeros((1, tq), F32)
    _, l0, _, l1 = _pipelined_tiles(n_last, logits, fold, (m_init, l_init, m_init, l_init))

    ot = acc_ref[0] * (1.0 / l0) - acc_ref[1] * (lam * (1.0 / l1))
    o = ot.T
    ms = jnp.mean(o * o, axis=-1, keepdims=True)
    y = (o * lax.rsqrt(ms + EPS) * g_ref[...]) * (1.0 - lam_init)
    o_ref[...] = y.astype(o_ref.dtype)


def _diff_attention(zqk, vt, diff_lambda, norm_g, slopes, batch, lam_init, tq=ATTN_TQ, tk=ATTN_TK):
    T = zqk.shape[0]
    S = T // batch
    H = DIFF_HEADS
    nq = S // tq
    assert S % tk == 0 and tk % tq == 0
    W = 2 * HEAD_DIM
    smem = pl.BlockSpec(memory_space=pltpu.SMEM)
    return pl.pallas_call(
        functools.partial(_diff_body, tq=tq, tk=tk, lam_init=lam_init),
        grid=(batch, H, nq),
        in_specs=[smem,
                  pl.BlockSpec((4, HEAD_DIM), lambda b, h, i: (0, 0)),
                  pl.BlockSpec((tq, W), lambda b, h, i: (b * nq + i, h)),
                  pl.BlockSpec((S, W), lambda b, h, i: (b, H + h)),
                  pl.BlockSpec((W, S), lambda b, h, i: (h, b)),
                  pl.BlockSpec((1, W), lambda b, h, i: (0, 0))],
        out_specs=pl.BlockSpec((tq, W), lambda b, h, i: (b * nq + i, h)),
        out_shape=jax.ShapeDtypeStruct((T, H * W), BF16),
        scratch_shapes=[pltpu.VMEM((2, W, tq), F32), pltpu.VMEM((2, 2, tk, tq), F32),
                        pltpu.VMEM((1 + tk // tq, tk, tq), F32)],
        compiler_params=_params("parallel", "parallel", "arbitrary"),
        name="diff_attention",
    )(slopes, diff_lambda.astype(F32), zqk, zqk, vt, norm_g.reshape(1, W).astype(F32))


def _moba_allowed(q, km, own, nb):
    nbp, nq = km.shape[0], q.shape[0]
    km_hi = km.astype(BF16)
    km_lo = (km - km_hi.astype(F32)).astype(BF16)
    gs = (lax.dot_general(km_hi, q, NT_DIMS, preferred_element_type=F32)
          + lax.dot_general(km_lo, q, NT_DIMS, preferred_element_type=F32))
    blk = lax.broadcasted_iota(jnp.int32, (nbp, nq), 0)
    rank = jnp.zeros((nbp, nq), F32)
    for m in range(nb):
        row = gs[m:m + 1, :]
        tie = jnp.where(blk > m, 1.0, 0.0)
        beats = jnp.where(row > gs, 1.0, jnp.where(row == gs, tie, 0.0))
        rank = rank + jnp.where(own > m, beats, 0.0)
    chosen = jnp.where(rank < float(min(MOBA_TOPK, nb)), 1.0, 0.0)
    return jnp.where(blk < own, chosen, jnp.where(blk == own, 1.0, 0.0))


def _moba_body(slopes_ref, q_ref, k_ref, vt_ref, o_ref, km_ref, ch_ref, acc_ref, s_ref, bias_ref, *, nb, tk, G):
    hp = pl.program_id(1)
    qi = pl.program_id(2)
    BLK = MOBA_BLOCK
    tq = q_ref.shape[0]
    bpt = tk // BLK
    scale = HEAD_DIM ** -0.5 * LOG2E
    slopes = [slopes_ref[hp * G + g] * LOG2E for g in range(G)]
    cols = [slice(g * HEAD_DIM, (g + 1) * HEAD_DIM) for g in range(G)]
    q0 = qi * tq
    n_last = q0 // tk
    last_bias = 1 + (q0 - n_last * tk) // tq
    own = (q0 + lax.broadcasted_iota(jnp.int32, (1, tq), 1)) // BLK

    @pl.when(qi == 0)
    def _():
        km_ref[...] = jnp.zeros(km_ref.shape, F32)
        for g in range(G):
            _alibi_bias_tiles(bias_ref.at[g], tk, tq, slopes[g])
            for n in range(nb):
                kb = k_ref[n * BLK:(n + 1) * BLK, cols[g]].astype(F32)
                km_ref[g, n:n + 1, :] = jnp.mean(kb, axis=0, keepdims=True)

    for g in range(G):
        ch_ref[g] = _moba_allowed(q_ref[:, cols[g]], km_ref[g], own, nb)
    acc_ref[...] = jnp.zeros(acc_ref.shape, F32)

    def logits(j, slot):
        k0 = pl.multiple_of(j * tk, tk)
        bidx = jnp.where(j == n_last, last_bias, 0)
        maxima = ()
        for g in range(G):
            t = lax.dot_general(k_ref[pl.ds(k0, tk), cols[g]], q_ref[:, cols[g]], NT_DIMS,
                                preferred_element_type=F32) * scale + bias_ref[g, bidx]
            parts = []
            for i in range(bpt):
                allowed = ch_ref[g, pl.ds(j * bpt + i, 1), :]
                parts.append(jnp.where(allowed > 0.5, t[i * BLK:(i + 1) * BLK], NEG))
            t = jnp.concatenate(parts, axis=0)
            s_ref[slot, g] = t
            maxima += (jnp.max(t, axis=0, keepdims=True),)
        return maxima

    def fold(j, slot, maxima, stats):
        out = ()
        for g in range(G):
            shift = -slopes[g] * (q0 - j * tk).astype(F32)
            vt = vt_ref[cols[g], pl.ds(pl.multiple_of(j * tk, tk), tk)]
            out += _softmax_fold(s_ref[slot, g], maxima[g], shift, vt, stats[2 * g], stats[2 * g + 1], acc_ref, g)
        return out

    stats = (jnp.full((1, tq), NEG, F32), jnp.zeros((1, tq), F32)) * G
    stats = _pipelined_tiles(n_last, logits, fold, stats)
    for g in range(G):
        o_ref[:, cols[g]] = (acc_ref[g] * (1.0 / stats[2 * g + 1])).T.astype(o_ref.dtype)


def _moba_attention(zqk, vt, slopes, batch, tq=ATTN_TQ, tk=ATTN_TK, G=MOBA_HEADS_PER_STEP):
    T = zqk.shape[0]
    S = T // batch
    H = MOBA_HEADS
    BLK = MOBA_BLOCK
    assert S % tk == 0 and tk % tq == 0 and tq % BLK == 0 and H % G == 0
    nb = S // BLK
    nq = S // tq
    nbp = -(-nb // 16) * 16
    W = G * HEAD_DIM
    smem = pl.BlockSpec(memory_space=pltpu.SMEM)
    return pl.pallas_call(
        functools.partial(_moba_body, nb=nb, tk=tk, G=G),
        grid=(batch, H // G, nq),
        in_specs=[smem,
                  pl.BlockSpec((tq, W), lambda b, h, i: (b * nq + i, h)),
                  pl.BlockSpec((S, W), lambda b, h, i: (b, H // G + h)),
                  pl.BlockSpec((W, S), lambda b, h, i: (h, b))],
        out_specs=pl.BlockSpec((tq, W), lambda b, h, i: (b * nq + i, h)),
        out_shape=jax.ShapeDtypeStruct((T, H * HEAD_DIM), BF16),
        scratch_shapes=[pltpu.VMEM((G, nbp, HEAD_DIM), F32), pltpu.VMEM((G, nbp, tq), F32),
                        pltpu.VMEM((G, HEAD_DIM, tq), F32), pltpu.VMEM((2, G, tk, tq), F32),
                        pltpu.VMEM((G, 1 + tk // tq, tk, tq), F32)],
        compiler_params=_params("parallel", "parallel", "arbitrary"),
        name="moba_attention",
    )(slopes, zqk, zqk, vt)


def _log_sigmoid(x):
    return jnp.minimum(x, 0.0) - jnp.log(1.0 + jnp.exp(-jnp.abs(x)))


def _split3(x):
    x1 = x.astype(BF16)
    r1 = x - x1.astype(F32)
    x2 = r1.astype(BF16)
    x3 = (r1 - x2.astype(F32)).astype(BF16)
    return x1, x2, x3


def _mlstm_body(qk_ref, v_ref, og_ref, ifc_ref, ifr_ref, cw_ref, cb_ref, gbc_ref, gbr_ref, ng_ref, out_ref,
                xext_ref, ct_ref, n_ref, m_ref, *, L):
    H, DK, DV = ML_HEADS, HEAD_DIM, ML_V
    PAD = SUBLANES
    c = pl.program_id(1)

    @pl.when(c == 0)
    def _():
        xext_ref[0:PAD, :] = jnp.zeros((PAD, xext_ref.shape[1]), F32)
        ct_ref[...] = jnp.zeros(ct_ref.shape, F32)
        n_ref[...] = jnp.zeros(n_ref.shape, F32)
        m_ref[...] = jnp.zeros(m_ref.shape, F32)

    xext_ref[PAD:PAD + L, :] = qk_ref[...].astype(F32)

    def conv_silu(col0):
        cs = slice(col0, col0 + DK)
        y = cb_ref[:, cs] + xext_ref[PAD:PAD + L, cs] * cw_ref[CONV_W - 1:CONV_W, cs]
        for j in range(CONV_W - 1):
            off = PAD - (CONV_W - 1) + j
            y = y + xext_ref[off:off + L, cs] * cw_ref[j:j + 1, cs]
        return y * jax.nn.sigmoid(y)

    rt = lax.broadcasted_iota(jnp.int32, (L, L), 0)
    cl = lax.broadcasted_iota(jnp.int32, (L, L), 1)
    tril = rt >= cl
    ones_tril = jnp.where(tril, 1.0, 0.0).astype(BF16)
    strict = rt > cl

    for h in range(H):
        qh = conv_silu(h * DK)
        kh = conv_silu(H * DK + h * DK) * (DK ** -0.5)
        qb = qh.astype(BF16)
        kb = kh.astype(BF16)
        vh = v_ref[:, h * DV:(h + 1) * DV]
        i_col = ifc_ref[:, h:h + 1] + gbc_ref[:, h:h + 1]
        lf_col = _log_sigmoid(ifc_ref[:, H + h:H + h + 1] + gbc_ref[:, H + h:H + h + 1])
        i_row = ifr_ref[h:h + 1, :] + gbr_ref[h:h + 1, :]
        b1, b2, b3 = _split3(jnp.where(strict, lf_col, 0.0))
        dp = (jnp.dot(ones_tril, b1, preferred_element_type=F32)
              + jnp.dot(ones_tril, b2, preferred_element_type=F32)
              + jnp.dot(ones_tril, b3, preferred_element_type=F32))
        g_col = dp[:, 0:1] + lf_col[0:1, :]
        g_last = g_col[L - 1:L, :]
        m_prev = m_ref[h][:, 0:1]
        d = jnp.where(tril, dp + i_row, NEG)
        inter = g_col + m_prev
        m_t = jnp.maximum(inter, jnp.max(d, axis=-1, keepdims=True))
        w = jnp.exp(d - m_t)
        a = jnp.exp(inter - m_t)
        sw = lax.dot_general(qb, kb, NT_DIMS, preferred_element_type=F32) * w
        num = (a * jnp.dot(qb, ct_ref[h].astype(BF16), preferred_element_type=F32)
               + jnp.dot(sw.astype(BF16), vh, preferred_element_type=F32))
        den = a * jnp.sum(qh * n_ref[h], axis=-1, keepdims=True) + jnp.sum(sw, axis=-1, keepdims=True)
        hh = num / jnp.maximum(jnp.abs(den), jnp.exp(-m_t))
        a_last = g_last - g_col + i_col
        m_new = jnp.maximum(g_last + m_prev, jnp.max(a_last, axis=0, keepdims=True))
        kw = kh * jnp.exp(a_last - m_new)
        decay = jnp.exp(g_last + m_prev - m_new)
        ct_ref[h] = decay * ct_ref[h] + lax.dot_general(kw.astype(BF16), vh, TN_DIMS, preferred_element_type=F32)
        n_ref[h] = decay * n_ref[h] + jnp.sum(kw, axis=0, keepdims=True)
        m_ref[h] = jnp.broadcast_to(m_new, (1, LANES))
        mu = jnp.mean(hh, axis=-1, keepdims=True)
        xc = hh - mu
        var = jnp.mean(xc * xc, axis=-1, keepdims=True)
        vs = slice(h * DV, (h + 1) * DV)
        yn = xc * lax.rsqrt(var + EPS) * ng_ref[:, vs]
        out_ref[:, vs] = (yn * jax.nn.sigmoid(og_ref[:, vs].astype(F32))).astype(out_ref.dtype)

    xext_ref[0:PAD, :] = xext_ref[L:L + PAD, :]


def _mlstm(zm, zif, conv_w, conv_b, gate_b, norm_g, batch, L=256):
    T = zm.shape[0]
    S = T // batch
    H = ML_HEADS
    L = min(L, S)
    nc = S // L
    W = 2 * H * HEAD_DIM
    zif_row = zif.reshape(batch, S, 2 * H).transpose(0, 2, 1)
    gb = gate_b.astype(F32).reshape(2 * H)
    row = lambda b, c: (b * nc + c, 0)
    return pl.pallas_call(
        functools.partial(_mlstm_body, L=L),
        grid=(batch, nc),
        in_specs=[pl.BlockSpec((L, W), row),
                  pl.BlockSpec((L, W), lambda b, c: (b * nc + c, 1)),
                  pl.BlockSpec((L, W), lambda b, c: (b * nc + c, 2)),
                  pl.BlockSpec((L, 2 * H), row),
                  pl.BlockSpec((None, 2 * H, L), lambda b, c: (b, 0, c)),
                  pl.BlockSpec((CONV_W, W), lambda b, c: (0, 0)),
                  pl.BlockSpec((1, W), lambda b, c: (0, 0)),
                  pl.BlockSpec((1, 2 * H), lambda b, c: (0, 0)),
                  pl.BlockSpec((2 * H, 1), lambda b, c: (0, 0)),
                  pl.BlockSpec((1, W), lambda b, c: (0, 0))],
        out_specs=pl.BlockSpec((L, W), row),
        out_shape=jax.ShapeDtypeStruct((T, W), BF16),
        scratch_shapes=[pltpu.VMEM((L + 2 * SUBLANES, W), F32),
                        pltpu.VMEM((H, HEAD_DIM, ML_V), F32),
                        pltpu.VMEM((H, 1, HEAD_DIM), F32),
                        pltpu.VMEM((H, 1, LANES), F32)],
        compiler_params=_params("parallel", "arbitrary"),
        name="mlstm",
    )(zm, zm, zm, zif, zif_row, conv_w.astype(F32), conv_b.reshape(1, W).astype(F32),
      gb.reshape(1, 2 * H), gb.reshape(2 * H, 1), norm_g.reshape(1, W).astype(F32))


def _alibi_slopes(n):
    return jnp.asarray(2.0 ** (-8.0 * jnp.arange(1, n + 1, dtype=F32) / n), dtype=F32)


def _split_w_in(w_in):
    c1 = 6 * DIFF_HEADS * DIFF_V
    c2 = c1 + 2 * ML_HEADS
    w_if = jnp.pad(w_in[:, :, c1:c2], ((0, 0), (0, 0), (0, LANES - 2 * ML_HEADS))).astype(BF16)
    return w_in.astype(BF16), w_if, w_in[:, :, c2:].astype(BF16)


def _mixer(xf, batch, l, norm1_g, w_in, diff_lambda, diff_norm_g, ml_conv_w, ml_conv_b, ml_gate_b, ml_norm_g,
           w_branch, w_out):
    BW = DIFF_HEADS * DIFF_V
    w_lo, w_if, w_hi = w_in
    h = _rmsnorm(xf, norm1_g, BF16)
    zd = _matmul(h, w_lo, l, BF16, PROJ_TM, PROJ_TN, cols=(0, 2 * BW), name="proj_diff_qk")
    vd = _matmul_t(h, w_lo, l, BF16, PROJ_TM, PROJ_TN, cols=(2 * BW, 3 * BW), name="proj_diff_v")
    zm = _matmul(h, w_lo, l, BF16, PROJ_TM, PROJ_TN, cols=(3 * BW, 6 * BW), name="proj_mlstm")
    zif = _matmul(h, w_if, l, F32, 1024, LANES, name="proj_if")[:, :2 * ML_HEADS]
    zb = _matmul(h, w_hi, l, BF16, PROJ_TM, PROJ_TN, cols=(0, 2 * BW), name="proj_moba_qk")
    vb = _matmul_t(h, w_hi, l, BF16, PROJ_TM, PROJ_TN, cols=(2 * BW, 3 * BW), name="proj_moba_v")
    zg = _matmul(h, w_hi, l, BF16, PROJ_TM, PROJ_TN, cols=(3 * BW, w_hi.shape[2]), name="proj_gates")

    lam_init = 0.8 - 0.6 * math.exp(-0.3 * l)
    ya = _diff_attention(zd, vd, diff_lambda, diff_norm_g, _alibi_slopes(DIFF_HEADS), batch, lam_init)
    yb = _mlstm(zm, zif, ml_conv_w, ml_conv_b, ml_gate_b, ml_norm_g, batch)
    yc = _moba_attention(zb, vb, _alibi_slopes(MOBA_HEADS), batch)
    merged = _merge(ya, yb, yc, w_branch, l, zg)
    return _matmul(merged, w_out, l, F32, 1024, 512, residual=xf, name="proj_out")


def _ffn(xf, l, norm2_g, w_gate_up, w_down):
    h = _rmsnorm(xf, norm2_g, BF16)
    act = _swiglu_up(h, w_gate_up, l)
    return _matmul(act, w_down, l, F32, 512, 256, residual=xf, name="ffn_down")


def kernel(x, norm1_g, w_in, diff_lambda, diff_norm_g, ml_conv_w, ml_conv_b, ml_gate_b, ml_norm_g, w_branch, w_out,
           norm2_g, w_gate_up, w_down, final_g):
    B, S, D = x.shape
    xf = x.reshape(B * S, D)
    w_in_b = _split_w_in(w_in)
    w_branch_b, w_out_b = w_branch.astype(BF16), w_out.astype(BF16)
    w_gate_up_b, w_down_b = w_gate_up.astype(BF16), w_down.astype(BF16)
    for l in range(w_in.shape[0]):
        xf = _mixer(xf, B, l, norm1_g[l], w_in_b, diff_lambda[l], diff_norm_g[l], ml_conv_w[l], ml_conv_b[l],
                    ml_gate_b[l], ml_norm_g[l], w_branch_b, w_out_b)
        xf = _ffn(xf, l, norm2_g[l], w_gate_up_b, w_down_b)
    return _rmsnorm(xf, final_g, F32).reshape(B, S, D)
```

```python
import functools
import math

import jax
import jax.numpy as jnp
from jax import lax
from jax.experimental import pallas as pl
from jax.experimental.pallas import tpu as pltpu

F32 = jnp.float32
BF16 = jnp.bfloat16

HEAD_DIM = 128
DIFF_HEADS = 8
DIFF_V = 2 * HEAD_DIM
ML_HEADS = 8
ML_V = 2 * HEAD_DIM
CONV_W = 4
MOBA_HEADS = 16
MOBA_BLOCK = 256
MOBA_TOPK = 3
N_BRANCH = 3
EPS = 1e-6
NEG = -1e30
LOG2E = math.log2(math.e)

LANES = 128
SUBLANES = 8
VMEM_LIMIT = 48 * 1024 * 1024

PROJ_TM = 1024
PROJ_TN = 512
ATTN_TQ = 512
ATTN_TK = 512
MOBA_HEADS_PER_STEP = 2

NT_DIMS = (((1,), (1,)), ((), ()))
TN_DIMS = (((0,), (0,)), ((), ()))


def _params(*sem):
    return pltpu.CompilerParams(dimension_semantics=sem, vmem_limit_bytes=VMEM_LIMIT)


def _rmsnorm_body(x_ref, g_ref, o_ref):
    x = x_ref[...]
    ms = jnp.mean(x * x, axis=-1, keepdims=True)
    o_ref[...] = (x * lax.rsqrt(ms + EPS) * g_ref[...]).astype(o_ref.dtype)


def _rmsnorm(x, g, out_dtype, tr=256):
    T, D = x.shape
    tr = min(tr, T)
    return pl.pallas_call(
        _rmsnorm_body,
        grid=(T // tr,),
        in_specs=[pl.BlockSpec((tr, D), lambda i: (i, 0)), pl.BlockSpec((1, D), lambda i: (0, 0))],
        out_specs=pl.BlockSpec((tr, D), lambda i: (i, 0)),
        out_shape=jax.ShapeDtypeStruct((T, D), out_dtype),
        compiler_params=_params("parallel"),
        name="rmsnorm",
    )(x, g.reshape(1, D).astype(F32))


def _mm_body(a_ref, b_ref, o_ref):
    o_ref[...] = jnp.dot(a_ref[...], b_ref[...], preferred_element_type=F32).astype(o_ref.dtype)


def _mm_res_body(a_ref, b_ref, r_ref, o_ref):
    o_ref[...] = (r_ref[...] + jnp.dot(a_ref[...], b_ref[...], preferred_element_type=F32)).astype(o_ref.dtype)


def _weight_spec(w, l, cols, tn):
    c0, c1 = cols if cols is not None else (0, w.shape[2])
    assert c0 % tn == 0 and (c1 - c0) % tn == 0
    return pl.BlockSpec((None, w.shape[1], tn), lambda i, j: (l, 0, c0 // tn + j)), c1 - c0


def _matmul(a, w, l, out_dtype, tm, tn, residual=None, cols=None, name="matmul"):
    M, K = a.shape
    tm = min(tm, M)
    w_spec, N = _weight_spec(w, l, cols, tn)
    assert M % tm == 0
    in_specs = [pl.BlockSpec((tm, K), lambda i, j: (i, 0)), w_spec]
    args = [a, w]
    body = _mm_body
    if residual is not None:
        in_specs.append(pl.BlockSpec((tm, tn), lambda i, j: (i, j)))
        args.append(residual)
        body = _mm_res_body
    return pl.pallas_call(
        body,
        grid=(M // tm, N // tn),
        in_specs=in_specs,
        out_specs=pl.BlockSpec((tm, tn), lambda i, j: (i, j)),
        out_shape=jax.ShapeDtypeStruct((M, N), out_dtype),
        compiler_params=_params("parallel", "parallel"),
        name=name,
    )(*args)


def _mm_t_body(a_ref, b_ref, o_ref):
    o_ref[...] = jnp.dot(a_ref[...], b_ref[...], preferred_element_type=F32).T.astype(o_ref.dtype)


def _matmul_t(a, w, l, out_dtype, tm, tn, cols=None, name="matmul_t"):
    M, K = a.shape
    tm = min(tm, M)
    w_spec, N = _weight_spec(w, l, cols, tn)
    assert M % tm == 0
    return pl.pallas_call(
        _mm_t_body,
        grid=(M // tm, N // tn),
        in_specs=[pl.BlockSpec((tm, K), lambda i, j: (i, 0)), w_spec],
        out_specs=pl.BlockSpec((tn, tm), lambda i, j: (j, i)),
        out_shape=jax.ShapeDtypeStruct((N, M), out_dtype),
        compiler_params=_params("parallel", "parallel"),
        name=name,
    )(a, w)


def _glu_body(a_ref, wg_ref, wu_ref, o_ref):
    a = a_ref[...]
    g = jnp.dot(a, wg_ref[...], preferred_element_type=F32)
    u = jnp.dot(a, wu_ref[...], preferred_element_type=F32)
    o_ref[...] = (g * jax.nn.sigmoid(g) * u).astype(o_ref.dtype)


def _swiglu_up(a, w_gu, l, tm=1024, tn=256):
    M, K = a.shape
    F = w_gu.shape[2] // 2
    tm = min(tm, M)
    nj = F // tn
    assert F % tn == 0 and M % tm == 0
    return pl.pallas_call(
        _glu_body,
        grid=(M // tm, nj),
        in_specs=[pl.BlockSpec((tm, K), lambda i, j: (i, 0)),
                  pl.BlockSpec((None, K, tn), lambda i, j: (l, 0, j)),
                  pl.BlockSpec((None, K, tn), lambda i, j: (l, 0, nj + j))],
        out_specs=pl.BlockSpec((tm, tn), lambda i, j: (i, j)),
        out_shape=jax.ShapeDtypeStruct((M, F), BF16),
        compiler_params=_params("parallel", "parallel"),
        name="swiglu_up",
    )(a, w_gu, w_gu)


def _merge_body(ya_ref, yb_ref, yc_ref, w0_ref, w1_ref, w2_ref, g0_ref, g1_ref, g2_ref, o_ref):
    acc = jax.nn.sigmoid(g0_ref[...].astype(F32)) * jnp.dot(ya_ref[...], w0_ref[...], preferred_element_type=F32)
    acc += jax.nn.sigmoid(g1_ref[...].astype(F32)) * jnp.dot(yb_ref[...], w1_ref[...], preferred_element_type=F32)
    acc += jax.nn.sigmoid(g2_ref[...].astype(F32)) * jnp.dot(yc_ref[...], w2_ref[...], preferred_element_type=F32)
    o_ref[...] = acc.astype(o_ref.dtype)


def _merge(ya, yb, yc, w_branch, l, gates, tm=512, tn=512):
    M, K = ya.shape
    D = w_branch.shape[3]
    tm, tn = min(tm, M), min(tn, D)
    nj = D // tn
    y_spec = pl.BlockSpec((tm, K), lambda i, j: (i, 0))
    w_specs = [pl.BlockSpec((None, None, K, tn), functools.partial(lambda i, j, b: (l, b, 0, j), b=b))
               for b in range(N_BRANCH)]
    g_specs = [pl.BlockSpec((tm, tn), functools.partial(lambda i, j, b: (i, b * nj + j), b=b)) for b in range(N_BRANCH)]
    return pl.pallas_call(
        _merge_body,
        grid=(M // tm, nj),
        in_specs=[y_spec, y_spec, y_spec] + w_specs + g_specs,
        out_specs=pl.BlockSpec((tm, tn), lambda i, j: (i, j)),
        out_shape=jax.ShapeDtypeStruct((M, D), BF16),
        compiler_params=_params("parallel", "parallel"),
        name="merge",
    )(ya, yb, yc, w_branch, w_branch, w_branch, gates, gates, gates)


def _softmax_fold(t, tmax, shift, vt, m_prev, l_prev, acc_ref, idx):
    m_new = jnp.maximum(m_prev, tmax + shift)
    p = jnp.exp2(t - (m_new - shift))
    alpha = jnp.exp2(m_prev - m_new)
    l_new = alpha * l_prev + jnp.sum(p, axis=0, keepdims=True)
    acc_ref[idx] = alpha * acc_ref[idx] + jnp.dot(vt, p.astype(BF16), preferred_element_type=F32)
    return m_new, l_new


def _pipelined_tiles(n_last, logits_fn, fold_fn, stats):
    def piped(j, carry, slot):
        stats, maxima = carry
        nxt = logits_fn(j + 1, 1 - slot)
        return fold_fn(j, slot, maxima, stats), nxt

    def body(j, carry):
        return lax.cond((j & 1) == 0, lambda c: piped(j, c, 0), lambda c: piped(j, c, 1), carry)

    carry = lax.fori_loop(0, n_last, body, (stats, logits_fn(0, 0)))
    return lax.cond((n_last & 1) == 0,
                    lambda c: fold_fn(n_last, 0, c[1], c[0]),
                    lambda c: fold_fn(n_last, 1, c[1], c[0]), carry)


def _alibi_bias_tiles(bias_ref, tk, tq, slope):
    off = lax.broadcasted_iota(jnp.int32, (tk, tq), 0) - lax.broadcasted_iota(jnp.int32, (tk, tq), 1)
    rel = off.astype(F32) * slope
    bias_ref[0] = rel
    for r in range(tk // tq):
        bias_ref[1 + r] = jnp.where(off <= r * tq, rel, NEG)


def _diff_body(slopes_ref, lam_ref, q_ref, k_ref, vt_ref, g_ref, o_ref, acc_ref, s_ref, bias_ref, *, tq, tk,
               lam_init):
    h = pl.program_id(1)
    qi = pl.program_id(2)
    slope = slopes_ref[h] * LOG2E
    scale = HEAD_DIM ** -0.5 * LOG2E
    lp = lam_ref[...]
    lam = (jnp.exp(jnp.sum(lp[0:1] * lp[1:2], axis=-1, keepdims=True))
           - jnp.exp(jnp.sum(lp[2:3] * lp[3:4], axis=-1, keepdims=True)) + lam_init)
    q0 = qi * tq
    n_last = q0 // tk
    last_bias = 1 + (q0 - n_last * tk) // tq

    @pl.when(qi == 0)
    def _():
        _alibi_bias_tiles(bias_ref, tk, tq, slope)

    acc_ref[...] = jnp.zeros(acc_ref.shape, F32)

    def logits(j, slot):
        k0 = pl.multiple_of(j * tk, tk)
        bias = bias_ref[jnp.where(j == n_last, last_bias, 0)]
        maxima = ()
        for c in range(2):
            q = q_ref[:, c * HEAD_DIM:(c + 1) * HEAD_DIM]
            k = k_ref[pl.ds(k0, tk), c * HEAD_DIM:(c + 1) * HEAD_DIM]
            t = lax.dot_general(k, q, NT_DIMS, preferred_element_type=F32) * scale + bias
            s_ref[slot, c] = t
            maxima += (jnp.max(t, axis=0, keepdims=True),)
        return maxima

    def fold(j, slot, maxima, stats):
        shift = -slope * (q0 - j * tk).astype(F32)
        vt = vt_ref[:, pl.ds(pl.multiple_of(j * tk, tk), tk)]
        out = ()
        for c in range(2):
            out += _softmax_fold(s_ref[slot, c], maxima[c], shift, vt, stats[2 * c], stats[2 * c + 1], acc_ref, c)
        return out

    m_init = jnp.full((1, tq), NEG, F32)
    l_init = jnp.zeros((1, tq), F32)
    _, l0, _, l1 = _pipelined_tiles(n_last, logits, fold, (m_init, l_init, m_init, l_init))

    ot = acc_ref[0] * (1.0 / l0) - acc_ref[1] * (lam * (1.0 / l1))
    o = ot.T
    ms = jnp.mean(o * o, axis=-1, keepdims=True)
    y = (o * lax.rsqrt(ms + EPS) * g_ref[...]) * (1.0 - lam_init)
    o_ref[...] = y.astype(o_ref.dtype)


def _diff_attention(zqk, vt, diff_lambda, norm_g, slopes, batch, lam_init, tq=ATTN_TQ, tk=ATTN_TK):
    T = zqk.shape[0]
    S = T // batch
    H = DIFF_HEADS
    nq = S // tq
    assert S % tk == 0 and tk % tq == 0
    W = 2 * HEAD_DIM
    smem = pl.BlockSpec(memory_space=pltpu.SMEM)
    return pl.pallas_call(
        functools.partial(_diff_body, tq=tq, tk=tk, lam_init=lam_init),
        grid=(batch, H, nq),
        in_specs=[smem,
                  pl.BlockSpec((4, HEAD_DIM), lambda b, h, i: (0, 0)),
                  pl.BlockSpec((tq, W), lambda b, h, i: (b * nq + i, h)),
                  pl.BlockSpec((S, W), lambda b, h, i: (b, H + h)),
                  pl.BlockSpec((W, S), lambda b, h, i: (h, b)),
                  pl.BlockSpec((1, W), lambda b, h, i: (0, 0))],
        out_specs=pl.BlockSpec((tq, W), lambda b, h, i: (b * nq + i, h)),
        out_shape=jax.ShapeDtypeStruct((T, H * W), BF16),
        scratch_shapes=[pltpu.VMEM((2, W, tq), F32), pltpu.VMEM((2, 2, tk, tq), F32),
                        pltpu.VMEM((1 + tk // tq, tk, tq), F32)],
        compiler_params=_params("parallel", "parallel", "arbitrary"),
        name="diff_attention",
    )(slopes, diff_lambda.astype(F32), zqk, zqk, vt, norm_g.reshape(1, W).astype(F32))


def _moba_allowed(q, km, own, nb):
    nbp, nq = km.shape[0], q.shape[0]
    km_hi = km.astype(BF16)
    km_lo = (km - km_hi.astype(F32)).astype(BF16)
    gs = (lax.dot_general(km_hi, q, NT_DIMS, preferred_element_type=F32)
          + lax.dot_general(km_lo, q, NT_DIMS, preferred_element_type=F32))
    blk = lax.broadcasted_iota(jnp.int32, (nbp, nq), 0)
    rank = jnp.zeros((nbp, nq), F32)
    for m in range(nb):
        row = gs[m:m + 1, :]
        tie = jnp.where(blk > m, 1.0, 0.0)
        beats = jnp.where(row > gs, 1.0, jnp.where(row == gs, tie, 0.0))
        rank = rank + jnp.where(own > m, beats, 0.0)
    chosen = jnp.where(rank < float(min(MOBA_TOPK, nb)), 1.0, 0.0)
    return jnp.where(blk < own, chosen, jnp.where(blk == own, 1.0, 0.0))


def _moba_body(slopes_ref, q_ref, k_ref, vt_ref, o_ref, km_ref, ch_ref, acc_ref, s_ref, bias_ref, *, nb, tk, G):
    hp = pl.program_id(1)
    qi = pl.program_id(2)
    BLK = MOBA_BLOCK
    tq = q_ref.shape[0]
    bpt = tk // BLK
    scale = HEAD_DIM ** -0.5 * LOG2E
    slopes = [slopes_ref[hp * G + g] * LOG2E for g in range(G)]
    cols = [slice(g * HEAD_DIM, (g + 1) * HEAD_DIM) for g in range(G)]
    q0 = qi * tq
    n_last = q0 // tk
    last_bias = 1 + (q0 - n_last * tk) // tq
    own = (q0 + lax.broadcasted_iota(jnp.int32, (1, tq), 1)) // BLK

    @pl.when(qi == 0)
    def _():
        km_ref[...] = jnp.zeros(km_ref.shape, F32)
        for g in range(G):
            _alibi_bias_tiles(bias_ref.at[g], tk, tq, slopes[g])
            for n in range(nb):
                kb = k_ref[n * BLK:(n + 1) * BLK, cols[g]].astype(F32)
                km_ref[g, n:n + 1, :] = jnp.mean(kb, axis=0, keepdims=True)

    for g in range(G):
        ch_ref[g] = _moba_allowed(q_ref[:, cols[g]], km_ref[g], own, nb)
    acc_ref[...] = jnp.zeros(acc_ref.shape, F32)

    def logits(j, slot):
        k0 = pl.multiple_of(j * tk, tk)
        bidx = jnp.where(j == n_last, last_bias, 0)
        maxima = ()
        for g in range(G):
            t = lax.dot_general(k_ref[pl.ds(k0, tk), cols[g]], q_ref[:, cols[g]], NT_DIMS,
                                preferred_element_type=F32) * scale + bias_ref[g, bidx]
            parts = []
            for i in range(bpt):
                allowed = ch_ref[g, pl.ds(j * bpt + i, 1), :]
                parts.append(jnp.where(allowed > 0.5, t[i * BLK:(i + 1) * BLK], NEG))
            t = jnp.concatenate(parts, axis=0)
            s_ref[slot, g] = t
            maxima += (jnp.max(t, axis=0, keepdims=True),)
        return maxima

    def fold(j, slot, maxima, stats):
        out = ()
        for g in range(G):
            shift = -slopes[g] * (q0 - j * tk).astype(F32)
            vt = vt_ref[cols[g], pl.ds(pl.multiple_of(j * tk, tk), tk)]
            out += _softmax_fold(s_ref[slot, g], maxima[g], shift, vt, stats[2 * g], stats[2 * g + 1], acc_ref, g)
        return out

    stats = (jnp.full((1, tq), NEG, F32), jnp.zeros((1, tq), F32)) * G
    stats = _pipelined_tiles(n_last, logits, fold, stats)
    for g in range(G):
        o_ref[:, cols[g]] = (acc_ref[g] * (1.0 / stats[2 * g + 1])).T.astype(o_ref.dtype)


def _moba_attention(zqk, vt, slopes, batch, tq=ATTN_TQ, tk=ATTN_TK, G=MOBA_HEADS_PER_STEP):
    T = zqk.shape[0]
    S = T // batch
    H = MOBA_HEADS
    BLK = MOBA_BLOCK
    assert S % tk == 0 and tk % tq == 0 and tq % BLK == 0 and H % G == 0
    nb = S // BLK
    nq = S // tq
    nbp = -(-nb // 16) * 16
    W = G * HEAD_DIM
    smem = pl.BlockSpec(memory_space=pltpu.SMEM)
    return pl.pallas_call(
        functools.partial(_moba_body, nb=nb, tk=tk, G=G),
        grid=(batch, H // G, nq),
        in_specs=[smem,
                  pl.BlockSpec((tq, W), lambda b, h, i: (b * nq + i, h)),
                  pl.BlockSpec((S, W), lambda b, h, i: (b, H // G + h)),
                  pl.BlockSpec((W, S), lambda b, h, i: (h, b))],
        out_specs=pl.BlockSpec((tq, W), lambda b, h, i: (b * nq + i, h)),
        out_shape=jax.ShapeDtypeStruct((T, H * HEAD_DIM), BF16),
        scratch_shapes=[pltpu.VMEM((G, nbp, HEAD_DIM), F32), pltpu.VMEM((G, nbp, tq), F32),
                        pltpu.VMEM((G, HEAD_DIM, tq), F32), pltpu.VMEM((2, G, tk, tq), F32),
                        pltpu.VMEM((G, 1 + tk // tq, tk, tq), F32)],
        compiler_params=_params("parallel", "parallel", "arbitrary"),
        name="moba_attention",
    )(slopes, zqk, zqk, vt)


def _log_sigmoid(x):
    return jnp.minimum(x, 0.0) - jnp.log(1.0 + jnp.exp(-jnp.abs(x)))


def _split3(x):
    x1 = x.astype(BF16)
    r1 = x - x1.astype(F32)
    x2 = r1.astype(BF16)
    x3 = (r1 - x2.astype(F32)).astype(BF16)
    return x1, x2, x3


def _mlstm_body(qk_ref, v_ref, og_ref, ifc_ref, ifr_ref, cw_ref, cb_ref, gbc_ref, gbr_ref, ng_ref, out_ref,
                xext_ref, ct_ref, n_ref, m_ref, *, L):
    H, DK, DV = ML_HEADS, HEAD_DIM, ML_V
    PAD = SUBLANES
    c = pl.program_id(1)

    @pl.when(c == 0)
    def _():
        xext_ref[0:PAD, :] = jnp.zeros((PAD, xext_ref.shape[1]), F32)
        ct_ref[...] = jnp.zeros(ct_ref.shape, F32)
        n_ref[...] = jnp.zeros(n_ref.shape, F32)
        m_ref[...] = jnp.zeros(m_ref.shape, F32)

    xext_ref[PAD:PAD + L, :] = qk_ref[...].astype(F32)

    def conv_silu(col0):
        cs = slice(col0, col0 + DK)
        y = cb_ref[:, cs] + xext_ref[PAD:PAD + L, cs] * cw_ref[CONV_W - 1:CONV_W, cs]
        for j in range(CONV_W - 1):
            off = PAD - (CONV_W - 1) + j
            y = y + xext_ref[off:off + L, cs] * cw_ref[j:j + 1, cs]
        return y * jax.nn.sigmoid(y)

    rt = lax.broadcasted_iota(jnp.int32, (L, L), 0)
    cl = lax.broadcasted_iota(jnp.int32, (L, L), 1)
    tril = rt >= cl
    ones_tril = jnp.where(tril, 1.0, 0.0).astype(BF16)
    strict = rt > cl

    for h in range(H):
        qh = conv_silu(h * DK)
        kh = conv_silu(H * DK + h * DK) * (DK ** -0.5)
        qb = qh.astype(BF16)
        kb = kh.astype(BF16)
        vh = v_ref[:, h * DV:(h + 1) * DV]
        i_col = ifc_ref[:, h:h + 1] + gbc_ref[:, h:h + 1]
        lf_col = _log_sigmoid(ifc_ref[:, H + h:H + h + 1] + gbc_ref[:, H + h:H + h + 1])
        i_row = ifr_ref[h:h + 1, :] + gbr_ref[h:h + 1, :]
        b1, b2, b3 = _split3(jnp.where(strict, lf_col, 0.0))
        dp = (jnp.dot(ones_tril, b1, preferred_element_type=F32)
              + jnp.dot(ones_tril, b2, preferred_element_type=F32)
              + jnp.dot(ones_tril, b3, preferred_element_type=F32))
        g_col = dp[:, 0:1] + lf_col[0:1, :]
        g_last = g_col[L - 1:L, :]
        m_prev = m_ref[h][:, 0:1]
        d = jnp.where(tril, dp + i_row, NEG)
        inter = g_col + m_prev
        m_t = jnp.maximum(inter, jnp.max(d, axis=-1, keepdims=True))
        w = jnp.exp(d - m_t)
        a = jnp.exp(inter - m_t)
        sw = lax.dot_general(qb, kb, NT_DIMS, preferred_element_type=F32) * w
        num = (a * jnp.dot(qb, ct_ref[h].astype(BF16), preferred_element_type=F32)
               + jnp.dot(sw.astype(BF16), vh, preferred_element_type=F32))
        den = a * jnp.sum(qh * n_ref[h], axis=-1, keepdims=True) + jnp.sum(sw, axis=-1, keepdims=True)
        hh = num / jnp.maximum(jnp.abs(den), jnp.exp(-m_t))
        a_last = g_last - g_col + i_col
        m_new = jnp.maximum(g_last + m_prev, jnp.max(a_last, axis=0, keepdims=True))
        kw = kh * jnp.exp(a_last - m_new)
        decay = jnp.exp(g_last + m_prev - m_new)
        ct_ref[h] = decay * ct_ref[h] + lax.dot_general(kw.astype(BF16), vh, TN_DIMS, preferred_element_type=F32)
        n_ref[h] = decay * n_ref[h] + jnp.sum(kw, axis=0, keepdims=True)
        m_ref[h] = jnp.broadcast_to(m_new, (1, LANES))
        mu = jnp.mean(hh, axis=-1, keepdims=True)
        xc = hh - mu
        var = jnp.mean(xc * xc, axis=-1, keepdims=True)
        vs = slice(h * DV, (h + 1) * DV)
        yn = xc * lax.rsqrt(var + EPS) * ng_ref[:, vs]
        out_ref[:, vs] = (yn * jax.nn.sigmoid(og_ref[:, vs].astype(F32))).astype(out_ref.dtype)

    xext_ref[0:PAD, :] = xext_ref[L:L + PAD, :]


def _mlstm(zm, zif, conv_w, conv_b, gate_b, norm_g, batch, L=256):
    T = zm.shape[0]
    S = T // batch
    H = ML_HEADS
    L = min(L, S)
    nc = S // L
    W = 2 * H * HEAD_DIM
    zif_row = zif.reshape(batch, S, 2 * H).transpose(0, 2, 1)
    gb = gate_b.astype(F32).reshape(2 * H)
    row = lambda b, c: (b * nc + c, 0)
    return pl.pallas_call(
        functools.partial(_mlstm_body, L=L),
        grid=(batch, nc),
        in_specs=[pl.BlockSpec((L, W), row),
                  pl.BlockSpec((L, W), lambda b, c: (b * nc + c, 1)),
                  pl.BlockSpec((L, W), lambda b, c: (b * nc + c, 2)),
                  pl.BlockSpec((L, 2 * H), row),
                  pl.BlockSpec((None, 2 * H, L), lambda b, c: (b, 0, c)),
                  pl.BlockSpec((CONV_W, W), lambda b, c: (0, 0)),
                  pl.BlockSpec((1, W), lambda b, c: (0, 0)),
                  pl.BlockSpec((1, 2 * H), lambda b, c: (0, 0)),
                  pl.BlockSpec((2 * H, 1), lambda b, c: (0, 0)),
                  pl.BlockSpec((1, W), lambda b, c: (0, 0))],
        out_specs=pl.BlockSpec((L, W), row),
        out_shape=jax.ShapeDtypeStruct((T, W), BF16),
        scratch_shapes=[pltpu.VMEM((L + 2 * SUBLANES, W), F32),
                        pltpu.VMEM((H, HEAD_DIM, ML_V), F32),
                        pltpu.VMEM((H, 1, HEAD_DIM), F32),
                        pltpu.VMEM((H, 1, LANES), F32)],
        compiler_params=_params("parallel", "arbitrary"),
        name="mlstm",
    )(zm, zm, zm, zif, zif_row, conv_w.astype(F32), conv_b.reshape(1, W).astype(F32),
      gb.reshape(1, 2 * H), gb.reshape(2 * H, 1), norm_g.reshape(1, W).astype(F32))


def _alibi_slopes(n):
    return jnp.asarray(2.0 ** (-8.0 * jnp.arange(1, n + 1, dtype=F32) / n), dtype=F32)


def _split_w_in_body(w_ref, lo_ref, if_ref, hi_ref, *, c1, c2):
    lo_ref[...] = w_ref[:, :c1].astype(lo_ref.dtype)
    gate_cols = w_ref[:, c1:c1 + LANES]
    lane = lax.broadcasted_iota(jnp.int32, gate_cols.shape, 1)
    if_ref[...] = jnp.where(lane < c2 - c1, gate_cols, 0.0).astype(if_ref.dtype)
    hi_ref[...] = w_ref[:, c2:].astype(hi_ref.dtype)


def _split_w_in(w_in, tr=64):
    L, K, N = w_in.shape
    c1 = 6 * DIFF_HEADS * DIFF_V
    c2 = c1 + 2 * ML_HEADS
    tr = min(tr, K)
    out_block = lambda n: pl.BlockSpec((None, tr, n), lambda l, i: (l, i, 0))
    return pl.pallas_call(
        functools.partial(_split_w_in_body, c1=c1, c2=c2),
        grid=(L, K // tr),
        in_specs=[out_block(N)],
        out_specs=[out_block(c1), out_block(LANES), out_block(N - c2)],
        out_shape=[jax.ShapeDtypeStruct((L, K, c1), BF16), jax.ShapeDtypeStruct((L, K, LANES), BF16),
                   jax.ShapeDtypeStruct((L, K, N - c2), BF16)],
        compiler_params=_params("parallel", "parallel"),
        name="split_w_in",
    )(w_in)


def _mixer(xf, batch, l, norm1_g, w_in, diff_lambda, diff_norm_g, ml_conv_w, ml_conv_b, ml_gate_b, ml_norm_g,
           w_branch, w_out):
    BW = DIFF_HEADS * DIFF_V
    w_lo, w_if, w_hi = w_in
    h = _rmsnorm(xf, norm1_g, BF16)
    zd = _matmul(h, w_lo, l, BF16, PROJ_TM, PROJ_TN, cols=(0, 2 * BW), name="proj_diff_qk")
    vd = _matmul_t(h, w_lo, l, BF16, PROJ_TM, PROJ_TN, cols=(2 * BW, 3 * BW), name="proj_diff_v")
    zm = _matmul(h, w_lo, l, BF16, PROJ_TM, PROJ_TN, cols=(3 * BW, 6 * BW), name="proj_mlstm")
    zif = _matmul(h, w_if, l, F32, 1024, LANES, name="proj_if")[:, :2 * ML_HEADS]
    zb = _matmul(h, w_hi, l, BF16, PROJ_TM, PROJ_TN, cols=(0, 2 * BW), name="proj_moba_qk")
    vb = _matmul_t(h, w_hi, l, BF16, PROJ_TM, PROJ_TN, cols=(2 * BW, 3 * BW), name="proj_moba_v")
    zg = _matmul(h, w_hi, l, BF16, PROJ_TM, PROJ_TN, cols=(3 * BW, w_hi.shape[2]), name="proj_gates")

    lam_init = 0.8 - 0.6 * math.exp(-0.3 * l)
    ya = _diff_attention(zd, vd, diff_lambda, diff_norm_g, _alibi_slopes(DIFF_HEADS), batch, lam_init)
    yb = _mlstm(zm, zif, ml_conv_w, ml_conv_b, ml_gate_b, ml_norm_g, batch)
    yc = _moba_attention(zb, vb, _alibi_slopes(MOBA_HEADS), batch)
    merged = _merge(ya, yb, yc, w_branch, l, zg)
    return _matmul(merged, w_out, l, F32, 1024, 512, residual=xf, name="proj_out")


def _ffn(xf, l, norm2_g, w_gate_up, w_down):
    h = _rmsnorm(xf, norm2_g, BF16)
    act = _swiglu_up(h, w_gate_up, l)
    return _matmul(act, w_down, l, F32, 512, 256, residual=xf, name="ffn_down")


def kernel(x, norm1_g, w_in, diff_lambda, diff_norm_g, ml_conv_w, ml_conv_b, ml_gate_b, ml_norm_g, w_branch, w_out,
           norm2_g, w_gate_up, w_down, final_g):
    B, S, D = x.shape
    xf = x.reshape(B * S, D)
    w_in_b = _split_w_in(w_in)
    w_branch_b, w_out_b = w_branch.astype(BF16), w_out.astype(BF16)
    w_gate_up_b, w_down_b = w_gate_up.astype(BF16), w_down.astype(BF16)
    for l in range(w_in.shape[0]):
        xf = _mixer(xf, B, l, norm1_g[l], w_in_b, diff_lambda[l], diff_norm_g[l], ml_conv_w[l], ml_conv_b[l],
                    ml_gate_b[l], ml_norm_g[l], w_branch_b, w_out_b)
        xf = _ffn(xf, l, norm2_g[l], w_gate_up_b, w_down_b)
    return _rmsnorm(xf, final_g, F32).reshape(B, S, D)
```

```python
import functools
import math

import jax
import jax.numpy as jnp
from jax import lax
from jax.experimental import pallas as pl
from jax.experimental.pallas import tpu as pltpu

F32 = jnp.float32
BF16 = jnp.bfloat16

HEAD_DIM = 128
DIFF_HEADS = 8
DIFF_V = 2 * HEAD_DIM
ML_HEADS = 8
ML_V = 2 * HEAD_DIM
CONV_W = 4
MOBA_HEADS = 16
MOBA_BLOCK = 256
MOBA_TOPK = 3
N_BRANCH = 3
EPS = 1e-6
NEG = -1e30
LOG2E = math.log2(math.e)

LANES = 128
SUBLANES = 8
VMEM_LIMIT = 48 * 1024 * 1024

PROJ_TM = 1024
PROJ_TN = 1024
ATTN_TQ = 512
ATTN_TK = 512
MOBA_HEADS_PER_STEP = 2

NT_DIMS = (((1,), (1,)), ((), ()))
TN_DIMS = (((0,), (0,)), ((), ()))


def _params(*sem):
    return pltpu.CompilerParams(dimension_semantics=sem, vmem_limit_bytes=VMEM_LIMIT)


def _rmsnorm_body(x_ref, g_ref, o_ref):
    x = x_ref[...]
    ms = jnp.mean(x * x, axis=-1, keepdims=True)
    o_ref[...] = (x * lax.rsqrt(ms + EPS) * g_ref[...]).astype(o_ref.dtype)


def _rmsnorm(x, g, out_dtype, tr=256):
    T, D = x.shape
    tr = min(tr, T)
    return pl.pallas_call(
        _rmsnorm_body,
        grid=(T // tr,),
        in_specs=[pl.BlockSpec((tr, D), lambda i: (i, 0)), pl.BlockSpec((1, D), lambda i: (0, 0))],
        out_specs=pl.BlockSpec((tr, D), lambda i: (i, 0)),
        out_shape=jax.ShapeDtypeStruct((T, D), out_dtype),
        compiler_params=_params("parallel"),
        name="rmsnorm",
    )(x, g.reshape(1, D).astype(F32))


def _mm_body(a_ref, b_ref, o_ref):
    o_ref[...] = jnp.dot(a_ref[...], b_ref[...], preferred_element_type=F32).astype(o_ref.dtype)


def _mm_res_body(a_ref, b_ref, r_ref, o_ref):
    o_ref[...] = (r_ref[...] + jnp.dot(a_ref[...], b_ref[...], preferred_element_type=F32)).astype(o_ref.dtype)


def _weight_spec(w, l, cols, tn):
    c0, c1 = cols if cols is not None else (0, w.shape[2])
    assert c0 % tn == 0 and (c1 - c0) % tn == 0
    return pl.BlockSpec((None, w.shape[1], tn), lambda i, j: (l, 0, c0 // tn + j)), c1 - c0


def _matmul(a, w, l, out_dtype, tm, tn, residual=None, cols=None, name="matmul"):
    M, K = a.shape
    tm = min(tm, M)
    w_spec, N = _weight_spec(w, l, cols, tn)
    assert M % tm == 0
    in_specs = [pl.BlockSpec((tm, K), lambda i, j: (i, 0)), w_spec]
    args = [a, w]
    body = _mm_body
    if residual is not None:
        in_specs.append(pl.BlockSpec((tm, tn), lambda i, j: (i, j)))
        args.append(residual)
        body = _mm_res_body
    return pl.pallas_call(
        body,
        grid=(M // tm, N // tn),
        in_specs=in_specs,
        out_specs=pl.BlockSpec((tm, tn), lambda i, j: (i, j)),
        out_shape=jax.ShapeDtypeStruct((M, N), out_dtype),
        compiler_params=_params("parallel", "parallel"),
        name=name,
    )(*args)


def _mm_t_body(a_ref, b_ref, o_ref):
    o_ref[...] = jnp.dot(a_ref[...], b_ref[...], preferred_element_type=F32).T.astype(o_ref.dtype)


def _matmul_t(a, w, l, out_dtype, tm, tn, cols=None, name="matmul_t"):
    M, K = a.shape
    tm = min(tm, M)
    w_spec, N = _weight_spec(w, l, cols, tn)
    assert M % tm == 0
    return pl.pallas_call(
        _mm_t_body,
        grid=(M // tm, N // tn),
        in_specs=[pl.BlockSpec((tm, K), lambda i, j: (i, 0)), w_spec],
        out_specs=pl.BlockSpec((tn, tm), lambda i, j: (j, i)),
        out_shape=jax.ShapeDtypeStruct((N, M), out_dtype),
        compiler_params=_params("parallel", "parallel"),
        name=name,
    )(a, w)


def _glu_body(a_ref, wg_ref, wu_ref, o_ref):
    a = a_ref[...]
    g = jnp.dot(a, wg_ref[...], preferred_element_type=F32)
    u = jnp.dot(a, wu_ref[...], preferred_element_type=F32)
    o_ref[...] = (g * jax.nn.sigmoid(g) * u).astype(o_ref.dtype)


def _swiglu_up(a, w_gu, l, tm=1024, tn=256):
    M, K = a.shape
    F = w_gu.shape[2] // 2
    tm = min(tm, M)
    nj = F // tn
    assert F % tn == 0 and M % tm == 0
    return pl.pallas_call(
        _glu_body,
        grid=(M // tm, nj),
        in_specs=[pl.BlockSpec((tm, K), lambda i, j: (i, 0)),
                  pl.BlockSpec((None, K, tn), lambda i, j: (l, 0, j)),
                  pl.BlockSpec((None, K, tn), lambda i, j: (l, 0, nj + j))],
        out_specs=pl.BlockSpec((tm, tn), lambda i, j: (i, j)),
        out_shape=jax.ShapeDtypeStruct((M, F), BF16),
        compiler_params=_params("parallel", "parallel"),
        name="swiglu_up",
    )(a, w_gu, w_gu)


def _merge_body(ya_ref, yb_ref, yc_ref, w0_ref, w1_ref, w2_ref, g0_ref, g1_ref, g2_ref, o_ref):
    acc = jax.nn.sigmoid(g0_ref[...].astype(F32)) * jnp.dot(ya_ref[...], w0_ref[...], preferred_element_type=F32)
    acc += jax.nn.sigmoid(g1_ref[...].astype(F32)) * jnp.dot(yb_ref[...], w1_ref[...], preferred_element_type=F32)
    acc += jax.nn.sigmoid(g2_ref[...].astype(F32)) * jnp.dot(yc_ref[...], w2_ref[...], preferred_element_type=F32)
    o_ref[...] = acc.astype(o_ref.dtype)


def _merge(ya, yb, yc, w_branch, l, gates, tm=512, tn=512):
    M, K = ya.shape
    D = w_branch.shape[3]
    tm, tn = min(tm, M), min(tn, D)
    nj = D // tn
    y_spec = pl.BlockSpec((tm, K), lambda i, j: (i, 0))
    w_specs = [pl.BlockSpec((None, None, K, tn), functools.partial(lambda i, j, b: (l, b, 0, j), b=b))
               for b in range(N_BRANCH)]
    g_specs = [pl.BlockSpec((tm, tn), functools.partial(lambda i, j, b: (i, b * nj + j), b=b)) for b in range(N_BRANCH)]
    return pl.pallas_call(
        _merge_body,
        grid=(M // tm, nj),
        in_specs=[y_spec, y_spec, y_spec] + w_specs + g_specs,
        out_specs=pl.BlockSpec((tm, tn), lambda i, j: (i, j)),
        out_shape=jax.ShapeDtypeStruct((M, D), BF16),
        compiler_params=_params("parallel", "parallel"),
        name="merge",
    )(ya, yb, yc, w_branch, w_branch, w_branch, gates, gates, gates)


def _softmax_fold(t, tmax, shift, vt, m_prev, l_prev, acc_ref, idx):
    m_new = jnp.maximum(m_prev, tmax + shift)
    p = jnp.exp2(t - (m_new - shift))
    alpha = jnp.exp2(m_prev - m_new)
    l_new = alpha * l_prev + jnp.sum(p, axis=0, keepdims=True)
    acc_ref[idx] = alpha * acc_ref[idx] + jnp.dot(vt, p.astype(BF16), preferred_element_type=F32)
    return m_new, l_new


def _pipelined_tiles(n_last, logits_fn, fold_fn, stats):
    def piped(j, carry, slot):
        stats, maxima = carry
        nxt = logits_fn(j + 1, 1 - slot)
        return fold_fn(j, slot, maxima, stats), nxt

    def body(j, carry):
        return lax.cond((j & 1) == 0, lambda c: piped(j, c, 0), lambda c: piped(j, c, 1), carry)

    carry = lax.fori_loop(0, n_last, body, (stats, logits_fn(0, 0)))
    return lax.cond((n_last & 1) == 0,
                    lambda c: fold_fn(n_last, 0, c[1], c[0]),
                    lambda c: fold_fn(n_last, 1, c[1], c[0]), carry)


def _alibi_bias_tiles(bias_ref, tk, tq, slope):
    off = lax.broadcasted_iota(jnp.int32, (tk, tq), 0) - lax.broadcasted_iota(jnp.int32, (tk, tq), 1)
    rel = off.astype(F32) * slope
    bias_ref[0] = rel
    for r in range(tk // tq):
        bias_ref[1 + r] = jnp.where(off <= r * tq, rel, NEG)


def _diff_body(slopes_ref, lam_ref, q_ref, k_ref, vt_ref, g_ref, o_ref, acc_ref, s_ref, bias_ref, *, tq, tk,
               lam_init):
    h = pl.program_id(1)
    qi = pl.program_id(2)
    slope = slopes_ref[h] * LOG2E
    scale = HEAD_DIM ** -0.5 * LOG2E
    lp = lam_ref[...]
    lam = (jnp.exp(jnp.sum(lp[0:1] * lp[1:2], axis=-1, keepdims=True))
           - jnp.exp(jnp.sum(lp[2:3] * lp[3:4], axis=-1, keepdims=True)) + lam_init)
    q0 = qi * tq
    n_last = q0 // tk
    last_bias = 1 + (q0 - n_last * tk) // tq

    @pl.when(qi == 0)
    def _():
        _alibi_bias_tiles(bias_ref, tk, tq, slope)

    acc_ref[...] = jnp.zeros(acc_ref.shape, F32)

    def logits(j, slot):
        k0 = pl.multiple_of(j * tk, tk)
        bias = bias_ref[jnp.where(j == n_last, last_bias, 0)]
        maxima = ()
        for c in range(2):
            q = q_ref[:, c * HEAD_DIM:(c + 1) * HEAD_DIM]
            k = k_ref[pl.ds(k0, tk), c * HEAD_DIM:(c + 1) * HEAD_DIM]
            t = lax.dot_general(k, q, NT_DIMS, preferred_element_type=F32) * scale + bias
            s_ref[slot, c] = t
            maxima += (jnp.max(t, axis=0, keepdims=True),)
        return maxima

    def fold(j, slot, maxima, stats):
        shift = -slope * (q0 - j * tk).astype(F32)
        vt = vt_ref[:, pl.ds(pl.multiple_of(j * tk, tk), tk)]
        out = ()
        for c in range(2):
            out += _softmax_fold(s_ref[slot, c], maxima[c], shift, vt, stats[2 * c], stats[2 * c + 1], acc_ref, c)
        return out

    m_init = jnp.full((1, tq), NEG, F32)
    l_init = jnp.zeros((1, tq), F32)
    _, l0, _, l1 = _pipelined_tiles(n_last, logits, fold, (m_init, l_init, m_init, l_init))

    ot = acc_ref[0] * (1.0 / l0) - acc_ref[1] * (lam * (1.0 / l1))
    o = ot.T
    ms = jnp.mean(o * o, axis=-1, keepdims=True)
    y = (o * lax.rsqrt(ms + EPS) * g_ref[...]) * (1.0 - lam_init)
    o_ref[...] = y.astype(o_ref.dtype)


def _diff_attention(zqk, vt, diff_lambda, norm_g, slopes, batch, lam_init, tq=ATTN_TQ, tk=ATTN_TK):
    T = zqk.shape[0]
    S = T // batch
    H = DIFF_HEADS
    nq = S // tq
    assert S % tk == 0 and tk % tq == 0
    W = 2 * HEAD_DIM
    smem = pl.BlockSpec(memory_space=pltpu.SMEM)
    return pl.pallas_call(
        functools.partial(_diff_body, tq=tq, tk=tk, lam_init=lam_init),
        grid=(batch, H, nq),
        in_specs=[smem,
                  pl.BlockSpec((4, HEAD_DIM), lambda b, h, i: (0, 0)),
                  pl.BlockSpec((tq, W), lambda b, h, i: (b * nq + i, h)),
                  pl.BlockSpec((S, W), lambda b, h, i: (b, H + h)),
                  pl.BlockSpec((W, S), lambda b, h, i: (h, b)),
                  pl.BlockSpec((1, W), lambda b, h, i: (0, 0))],
        out_specs=pl.BlockSpec((tq, W), lambda b, h, i: (b * nq + i, h)),
        out_shape=jax.ShapeDtypeStruct((T, H * W), BF16),
        scratch_shapes=[pltpu.VMEM((2, W, tq), F32), pltpu.VMEM((2, 2, tk, tq), F32),
                        pltpu.VMEM((1 + tk // tq, tk, tq), F32)],
        compiler_params=_params("parallel", "parallel", "arbitrary"),
        name="diff_attention",
    )(slopes, diff_lambda.astype(F32), zqk, zqk, vt, norm_g.reshape(1, W).astype(F32))


def _moba_allowed(q, km, own, nb):
    nbp, nq = km.shape[0], q.shape[0]
    km_hi = km.astype(BF16)
    km_lo = (km - km_hi.astype(F32)).astype(BF16)
    gs = (lax.dot_general(km_hi, q, NT_DIMS, preferred_element_type=F32)
          + lax.dot_general(km_lo, q, NT_DIMS, preferred_element_type=F32))
    blk = lax.broadcasted_iota(jnp.int32, (nbp, nq), 0)
    rank = jnp.zeros((nbp, nq), F32)
    for m in range(nb):
        row = gs[m:m + 1, :]
        tie = jnp.where(blk > m, 1.0, 0.0)
        beats = jnp.where(row > gs, 1.0, jnp.where(row == gs, tie, 0.0))
        rank = rank + jnp.where(own > m, beats, 0.0)
    chosen = jnp.where(rank < float(min(MOBA_TOPK, nb)), 1.0, 0.0)
    return jnp.where(blk < own, chosen, jnp.where(blk == own, 1.0, 0.0))


def _moba_body(slopes_ref, q_ref, k_ref, vt_ref, o_ref, km_ref, ch_ref, acc_ref, s_ref, bias_ref, *, nb, tk, G):
    hp = pl.program_id(1)
    qi = pl.program_id(2)
    BLK = MOBA_BLOCK
    tq = q_ref.shape[0]
    bpt = tk // BLK
    scale = HEAD_DIM ** -0.5 * LOG2E
    slopes = [slopes_ref[hp * G + g] * LOG2E for g in range(G)]
    cols = [slice(g * HEAD_DIM, (g + 1) * HEAD_DIM) for g in range(G)]
    q0 = qi * tq
    n_last = q0 // tk
    last_bias = 1 + (q0 - n_last * tk) // tq
    own = (q0 + lax.broadcasted_iota(jnp.int32, (1, tq), 1)) // BLK

    @pl.when(qi == 0)
    def _():
        km_ref[...] = jnp.zeros(km_ref.shape, F32)
        for g in range(G):
            _alibi_bias_tiles(bias_ref.at[g], tk, tq, slopes[g])
            for n in range(nb):
                kb = k_ref[n * BLK:(n + 1) * BLK, cols[g]].astype(F32)
                km_ref[g, n:n + 1, :] = jnp.mean(kb, axis=0, keepdims=True)

    for g in range(G):
        ch_ref[g] = _moba_allowed(q_ref[:, cols[g]], km_ref[g], own, nb)
    acc_ref[...] = jnp.zeros(acc_ref.shape, F32)

    def logits(j, slot):
        k0 = pl.multiple_of(j * tk, tk)
        bidx = jnp.where(j == n_last, last_bias, 0)
        maxima = ()
        for g in range(G):
            t = lax.dot_general(k_ref[pl.ds(k0, tk), cols[g]], q_ref[:, cols[g]], NT_DIMS,
                                preferred_element_type=F32) * scale + bias_ref[g, bidx]
            parts = []
            for i in range(bpt):
                allowed = ch_ref[g, pl.ds(j * bpt + i, 1), :]
                parts.append(jnp.where(allowed > 0.5, t[i * BLK:(i + 1) * BLK], NEG))
            t = jnp.concatenate(parts, axis=0)
            s_ref[slot, g] = t
            maxima += (jnp.max(t, axis=0, keepdims=True),)
        return maxima

    def fold(j, slot, maxima, stats):
        out = ()
        for g in range(G):
            shift = -slopes[g] * (q0 - j * tk).astype(F32)
            vt = vt_ref[cols[g], pl.ds(pl.multiple_of(j * tk, tk), tk)]
            out += _softmax_fold(s_ref[slot, g], maxima[g], shift, vt, stats[2 * g], stats[2 * g + 1], acc_ref, g)
        return out

    stats = (jnp.full((1, tq), NEG, F32), jnp.zeros((1, tq), F32)) * G
    stats = _pipelined_tiles(n_last, logits, fold, stats)
    for g in range(G):
        o_ref[:, cols[g]] = (acc_ref[g] * (1.0 / stats[2 * g + 1])).T.astype(o_ref.dtype)


def _moba_attention(zqk, vt, slopes, batch, tq=ATTN_TQ, tk=ATTN_TK, G=MOBA_HEADS_PER_STEP):
    T = zqk.shape[0]
    S = T // batch
    H = MOBA_HEADS
    BLK = MOBA_BLOCK
    assert S % tk == 0 and tk % tq == 0 and tq % BLK == 0 and H % G == 0
    nb = S // BLK
    nq = S // tq
    nbp = -(-nb // 16) * 16
    W = G * HEAD_DIM
    smem = pl.BlockSpec(memory_space=pltpu.SMEM)
    return pl.pallas_call(
        functools.partial(_moba_body, nb=nb, tk=tk, G=G),
        grid=(batch, H // G, nq),
        in_specs=[smem,
                  pl.BlockSpec((tq, W), lambda b, h, i: (b * nq + i, h)),
                  pl.BlockSpec((S, W), lambda b, h, i: (b, H // G + h)),
                  pl.BlockSpec((W, S), lambda b, h, i: (h, b))],
        out_specs=pl.BlockSpec((tq, W), lambda b, h, i: (b * nq + i, h)),
        out_shape=jax.ShapeDtypeStruct((T, H * HEAD_DIM), BF16),
        scratch_shapes=[pltpu.VMEM((G, nbp, HEAD_DIM), F32), pltpu.VMEM((G, nbp, tq), F32),
                        pltpu.VMEM((G, HEAD_DIM, tq), F32), pltpu.VMEM((2, G, tk, tq), F32),
                        pltpu.VMEM((G, 1 + tk // tq, tk, tq), F32)],
        compiler_params=_params("parallel", "parallel", "arbitrary"),
        name="moba_attention",
    )(slopes, zqk, zqk, vt)


def _log_sigmoid(x):
    return jnp.minimum(x, 0.0) - jnp.log(1.0 + jnp.exp(-jnp.abs(x)))


def _split3(x):
    x1 = x.astype(BF16)
    r1 = x - x1.astype(F32)
    x2 = r1.astype(BF16)
    x3 = (r1 - x2.astype(F32)).astype(BF16)
    return x1, x2, x3


def _mlstm_body(qk_ref, v_ref, og_ref, ifc_ref, ifr_ref, cw_ref, cb_ref, gbc_ref, gbr_ref, ng_ref, out_ref,
                xext_ref, ct_ref, n_ref, m_ref, *, L):
    H, DK, DV = ML_HEADS, HEAD_DIM, ML_V
    PAD = SUBLANES
    c = pl.program_id(1)

    @pl.when(c == 0)
    def _():
        xext_ref[0:PAD, :] = jnp.zeros((PAD, xext_ref.shape[1]), F32)
        ct_ref[...] = jnp.zeros(ct_ref.shape, F32)
        n_ref[...] = jnp.zeros(n_ref.shape, F32)
        m_ref[...] = jnp.zeros(m_ref.shape, F32)

    xext_ref[PAD:PAD + L, :] = qk_ref[...].astype(F32)

    def conv_silu(col0):
        cs = slice(col0, col0 + DK)
        y = cb_ref[:, cs] + xext_ref[PAD:PAD + L, cs] * cw_ref[CONV_W - 1:CONV_W, cs]
        for j in range(CONV_W - 1):
            off = PAD - (CONV_W - 1) + j
            y = y + xext_ref[off:off + L, cs] * cw_ref[j:j + 1, cs]
        return y * jax.nn.sigmoid(y)

    rt = lax.broadcasted_iota(jnp.int32, (L, L), 0)
    cl = lax.broadcasted_iota(jnp.int32, (L, L), 1)
    tril = rt >= cl
    ones_tril = jnp.where(tril, 1.0, 0.0).astype(BF16)
    strict = rt > cl

    for h in range(H):
        qh = conv_silu(h * DK)
        kh = conv_silu(H * DK + h * DK) * (DK ** -0.5)
        qb = qh.astype(BF16)
        kb = kh.astype(BF16)
        vh = v_ref[:, h * DV:(h + 1) * DV]
        i_col = ifc_ref[:, h:h + 1] + gbc_ref[:, h:h + 1]
        lf_col = _log_sigmoid(ifc_ref[:, H + h:H + h + 1] + gbc_ref[:, H + h:H + h + 1])
        i_row = ifr_ref[h:h + 1, :] + gbr_ref[h:h + 1, :]
        b1, b2, b3 = _split3(jnp.where(strict, lf_col, 0.0))
        dp = (jnp.dot(ones_tril, b1, preferred_element_type=F32)
              + jnp.dot(ones_tril, b2, preferred_element_type=F32)
              + jnp.dot(ones_tril, b3, preferred_element_type=F32))
        g_col = dp[:, 0:1] + lf_col[0:1, :]
        g_last = g_col[L - 1:L, :]
        m_prev = m_ref[h][:, 0:1]
        d = jnp.where(tril, dp + i_row, NEG)
        inter = g_col + m_prev
        m_t = jnp.maximum(inter, jnp.max(d, axis=-1, keepdims=True))
        w = jnp.exp(d - m_t)
        a = jnp.exp(inter - m_t)
        sw = lax.dot_general(qb, kb, NT_DIMS, preferred_element_type=F32) * w
        num = (a * jnp.dot(qb, ct_ref[h].astype(BF16), preferred_element_type=F32)
               + jnp.dot(sw.astype(BF16), vh, preferred_element_type=F32))
        den = a * jnp.sum(qh * n_ref[h], axis=-1, keepdims=True) + jnp.sum(sw, axis=-1, keepdims=True)
        hh = num / jnp.maximum(jnp.abs(den), jnp.exp(-m_t))
        a_last = g_last - g_col + i_col
        m_new = jnp.maximum(g_last + m_prev, jnp.max(a_last, axis=0, keepdims=True))
        kw = kh * jnp.exp(a_last - m_new)
        decay = jnp.exp(g_last + m_prev - m_new)
        ct_ref[h] = decay * ct_ref[h] + lax.dot_general(kw.astype(BF16), vh, TN_DIMS, preferred_element_type=F32)
        n_ref[h] = decay * n_ref[h] + jnp.sum(kw, axis=0, keepdims=True)
        m_ref[h] = jnp.broadcast_to(m_new, (1, LANES))
        mu = jnp.mean(hh, axis=-1, keepdims=True)
        xc = hh - mu
        var = jnp.mean(xc * xc, axis=-1, keepdims=True)
        vs = slice(h * DV, (h + 1) * DV)
        yn = xc * lax.rsqrt(var + EPS) * ng_ref[:, vs]
        out_ref[:, vs] = (yn * jax.nn.sigmoid(og_ref[:, vs].astype(F32))).astype(out_ref.dtype)

    xext_ref[0:PAD, :] = xext_ref[L:L + PAD, :]


def _mlstm(zm, zif, conv_w, conv_b, gate_b, norm_g, batch, L=256):
    T = zm.shape[0]
    S = T // batch
    H = ML_HEADS
    L = min(L, S)
    nc = S // L
    W = 2 * H * HEAD_DIM
    zif_row = zif.reshape(batch, S, 2 * H).transpose(0, 2, 1)
    gb = gate_b.astype(F32).reshape(2 * H)
    row = lambda b, c: (b * nc + c, 0)
    return pl.pallas_call(
        functools.partial(_mlstm_body, L=L),
        grid=(batch, nc),
        in_specs=[pl.BlockSpec((L, W), row),
                  pl.BlockSpec((L, W), lambda b, c: (b * nc + c, 1)),
                  pl.BlockSpec((L, W), lambda b, c: (b * nc + c, 2)),
                  pl.BlockSpec((L, 2 * H), row),
                  pl.BlockSpec((None, 2 * H, L), lambda b, c: (b, 0, c)),
                  pl.BlockSpec((CONV_W, W), lambda b, c: (0, 0)),
                  pl.BlockSpec((1, W), lambda b, c: (0, 0)),
                  pl.BlockSpec((1, 2 * H), lambda b, c: (0, 0)),
                  pl.BlockSpec((2 * H, 1), lambda b, c: (0, 0)),
                  pl.BlockSpec((1, W), lambda b, c: (0, 0))],
        out_specs=pl.BlockSpec((L, W), row),
        out_shape=jax.ShapeDtypeStruct((T, W), BF16),
        scratch_shapes=[pltpu.VMEM((L + 2 * SUBLANES, W), F32),
                        pltpu.VMEM((H, HEAD_DIM, ML_V), F32),
                        pltpu.VMEM((H, 1, HEAD_DIM), F32),
                        pltpu.VMEM((H, 1, LANES), F32)],
        compiler_params=_params("parallel", "arbitrary"),
        name="mlstm",
    )(zm, zm, zm, zif, zif_row, conv_w.astype(F32), conv_b.reshape(1, W).astype(F32),
      gb.reshape(1, 2 * H), gb.reshape(2 * H, 1), norm_g.reshape(1, W).astype(F32))


def _alibi_slopes(n):
    return jnp.asarray(2.0 ** (-8.0 * jnp.arange(1, n + 1, dtype=F32) / n), dtype=F32)


def _split_w_in(w_in):
    c1 = 6 * DIFF_HEADS * DIFF_V
    c2 = c1 + 2 * ML_HEADS
    w_if = jnp.pad(w_in[:, :, c1:c2], ((0, 0), (0, 0), (0, LANES - 2 * ML_HEADS))).astype(BF16)
    return w_in.astype(BF16), w_if, w_in[:, :, c2:].astype(BF16)


def _mixer(xf, batch, l, norm1_g, w_in, diff_lambda, diff_norm_g, ml_conv_w, ml_conv_b, ml_gate_b, ml_norm_g,
           w_branch, w_out):
    BW = DIFF_HEADS * DIFF_V
    w_lo, w_if, w_hi = w_in
    h = _rmsnorm(xf, norm1_g, BF16)
    zd = _matmul(h, w_lo, l, BF16, PROJ_TM, PROJ_TN, cols=(0, 2 * BW), name="proj_diff_qk")
    vd = _matmul_t(h, w_lo, l, BF16, PROJ_TM, PROJ_TN, cols=(2 * BW, 3 * BW), name="proj_diff_v")
    zm = _matmul(h, w_lo, l, BF16, PROJ_TM, PROJ_TN, cols=(3 * BW, 6 * BW), name="proj_mlstm")
    zif = _matmul(h, w_if, l, F32, 1024, LANES, name="proj_if")[:, :2 * ML_HEADS]
    zb = _matmul(h, w_hi, l, BF16, PROJ_TM, PROJ_TN, cols=(0, 2 * BW), name="proj_moba_qk")
    vb = _matmul_t(h, w_hi, l, BF16, PROJ_TM, PROJ_TN, cols=(2 * BW, 3 * BW), name="proj_moba_v")
    zg = _matmul(h, w_hi, l, BF16, PROJ_TM, PROJ_TN, cols=(3 * BW, w_hi.shape[2]), name="proj_gates")

    lam_init = 0.8 - 0.6 * math.exp(-0.3 * l)
    ya = _diff_attention(zd, vd, diff_lambda, diff_norm_g, _alibi_slopes(DIFF_HEADS), batch, lam_init)
    yb = _mlstm(zm, zif, ml_conv_w, ml_conv_b, ml_gate_b, ml_norm_g, batch)
    yc = _moba_attention(zb, vb, _alibi_slopes(MOBA_HEADS), batch)
    merged = _merge(ya, yb, yc, w_branch, l, zg)
    return _matmul(merged, w_out, l, F32, 1024, 512, residual=xf, name="proj_out")


def _ffn(xf, l, norm2_g, w_gate_up, w_down):
    h = _rmsnorm(xf, norm2_g, BF16)
    act = _swiglu_up(h, w_gate_up, l)
    return _matmul(act, w_down, l, F32, 512, 256, residual=xf, name="ffn_down")


def kernel(x, norm1_g, w_in, diff_lambda, diff_norm_g, ml_conv_w, ml_conv_b, ml_gate_b, ml_norm_g, w_branch, w_out,
           norm2_g, w_gate_up, w_down, final_g):
    B, S, D = x.shape
    xf = x.reshape(B * S, D)
    w_in_b = _split_w_in(w_in)
    w_branch_b, w_out_b = w_branch.astype(BF16), w_out.astype(BF16)
    w_gate_up_b, w_down_b = w_gate_up.astype(BF16), w_down.astype(BF16)
    for l in range(w_in.shape[0]):
        xf = _mixer(xf, B, l, norm1_g[l], w_in_b, diff_lambda[l], diff_norm_g[l], ml_conv_w[l], ml_conv_b[l],
                    ml_gate_b[l], ml_norm_g[l], w_branch_b, w_out_b)
        xf = _ffn(xf, l, norm2_g[l], w_gate_up_b, w_down_b)
    return _rmsnorm(xf, final_g, F32).reshape(B, S, D)
```

```python
import functools
import math

import jax
import jax.numpy as jnp
from jax import lax
from jax.experimental import pallas as pl
from jax.experimental.pallas import tpu as pltpu

F32 = jnp.float32
BF16 = jnp.bfloat16

HEAD_DIM = 128
DIFF_HEADS = 8
DIFF_V = 2 * HEAD_DIM
ML_HEADS = 8
ML_V = 2 * HEAD_DIM
CONV_W = 4
MOBA_HEADS = 16
MOBA_BLOCK = 256
MOBA_TOPK = 3
N_BRANCH = 3
EPS = 1e-6
NEG = -1e30
LOG2E = math.log2(math.e)

LANES = 128
SUBLANES = 8
VMEM_LIMIT = 48 * 1024 * 1024

PROJ_TM = 1024
PROJ_TN = 1024
ATTN_TQ = 512
ATTN_TK = 512
MOBA_HEADS_PER_STEP = 4
DIFF_HEADS_PER_STEP = 2

NT_DIMS = (((1,), (1,)), ((), ()))
TN_DIMS = (((0,), (0,)), ((), ()))


def _params(*sem):
    return pltpu.CompilerParams(dimension_semantics=sem, vmem_limit_bytes=VMEM_LIMIT)


def _rmsnorm_body(x_ref, g_ref, o_ref):
    x = x_ref[...]
    ms = jnp.mean(x * x, axis=-1, keepdims=True)
    o_ref[...] = (x * lax.rsqrt(ms + EPS) * g_ref[...]).astype(o_ref.dtype)


def _rmsnorm(x, g, out_dtype, tr=256):
    T, D = x.shape
    tr = min(tr, T)
    return pl.pallas_call(
        _rmsnorm_body,
        grid=(T // tr,),
        in_specs=[pl.BlockSpec((tr, D), lambda i: (i, 0)), pl.BlockSpec((1, D), lambda i: (0, 0))],
        out_specs=pl.BlockSpec((tr, D), lambda i: (i, 0)),
        out_shape=jax.ShapeDtypeStruct((T, D), out_dtype),
        compiler_params=_params("parallel"),
        name="rmsnorm",
    )(x, g.reshape(1, D).astype(F32))


def _mm_body(a_ref, b_ref, o_ref):
    o_ref[...] = jnp.dot(a_ref[...], b_ref[...], preferred_element_type=F32).astype(o_ref.dtype)


def _mm_res_body(a_ref, b_ref, r_ref, o_ref):
    o_ref[...] = (r_ref[...] + jnp.dot(a_ref[...], b_ref[...], preferred_element_type=F32)).astype(o_ref.dtype)


def _weight_spec(w, l, cols, tn):
    c0, c1 = cols if cols is not None else (0, w.shape[2])
    assert c0 % tn == 0 and (c1 - c0) % tn == 0
    return pl.BlockSpec((None, w.shape[1], tn), lambda i, j: (l, 0, c0 // tn + j)), c1 - c0


def _matmul(a, w, l, out_dtype, tm, tn, residual=None, cols=None, name="matmul"):
    M, K = a.shape
    tm = min(tm, M)
    w_spec, N = _weight_spec(w, l, cols, tn)
    assert M % tm == 0
    in_specs = [pl.BlockSpec((tm, K), lambda i, j: (i, 0)), w_spec]
    args = [a, w]
    body = _mm_body
    if residual is not None:
        in_specs.append(pl.BlockSpec((tm, tn), lambda i, j: (i, j)))
        args.append(residual)
        body = _mm_res_body
    return pl.pallas_call(
        body,
        grid=(M // tm, N // tn),
        in_specs=in_specs,
        out_specs=pl.BlockSpec((tm, tn), lambda i, j: (i, j)),
        out_shape=jax.ShapeDtypeStruct((M, N), out_dtype),
        compiler_params=_params("parallel", "parallel"),
        name=name,
    )(*args)


def _mm_t_body(a_ref, b_ref, o_ref):
    o_ref[...] = jnp.dot(a_ref[...], b_ref[...], preferred_element_type=F32).T.astype(o_ref.dtype)


def _matmul_t(a, w, l, out_dtype, tm, tn, cols=None, name="matmul_t"):
    M, K = a.shape
    tm = min(tm, M)
    w_spec, N = _weight_spec(w, l, cols, tn)
    assert M % tm == 0
    return pl.pallas_call(
        _mm_t_body,
        grid=(M // tm, N // tn),
        in_specs=[pl.BlockSpec((tm, K), lambda i, j: (i, 0)), w_spec],
        out_specs=pl.BlockSpec((tn, tm), lambda i, j: (j, i)),
        out_shape=jax.ShapeDtypeStruct((N, M), out_dtype),
        compiler_params=_params("parallel", "parallel"),
        name=name,
    )(a, w)


def _glu_body(a_ref, wg_ref, wu_ref, o_ref):
    a = a_ref[...]
    g = jnp.dot(a, wg_ref[...], preferred_element_type=F32)
    u = jnp.dot(a, wu_ref[...], preferred_element_type=F32)
    o_ref[...] = (g * jax.nn.sigmoid(g) * u).astype(o_ref.dtype)


def _swiglu_up(a, w_gu, l, tm=1024, tn=256):
    M, K = a.shape
    F = w_gu.shape[2] // 2
    tm = min(tm, M)
    nj = F // tn
    assert F % tn == 0 and M % tm == 0
    return pl.pallas_call(
        _glu_body,
        grid=(M // tm, nj),
        in_specs=[pl.BlockSpec((tm, K), lambda i, j: (i, 0)),
                  pl.BlockSpec((None, K, tn), lambda i, j: (l, 0, j)),
                  pl.BlockSpec((None, K, tn), lambda i, j: (l, 0, nj + j))],
        out_specs=pl.BlockSpec((tm, tn), lambda i, j: (i, j)),
        out_shape=jax.ShapeDtypeStruct((M, F), BF16),
        compiler_params=_params("parallel", "parallel"),
        name="swiglu_up",
    )(a, w_gu, w_gu)


def _merge_body(ya_ref, yb_ref, yc_ref, w0_ref, w1_ref, w2_ref, g0_ref, g1_ref, g2_ref, o_ref):
    acc = jax.nn.sigmoid(g0_ref[...].astype(F32)) * jnp.dot(ya_ref[...], w0_ref[...], preferred_element_type=F32)
    acc += jax.nn.sigmoid(g1_ref[...].astype(F32)) * jnp.dot(yb_ref[...], w1_ref[...], preferred_element_type=F32)
    acc += jax.nn.sigmoid(g2_ref[...].astype(F32)) * jnp.dot(yc_ref[...], w2_ref[...], preferred_element_type=F32)
    o_ref[...] = acc.astype(o_ref.dtype)


def _merge(ya, yb, yc, w_branch, l, gates, tm=512, tn=512):
    M, K = ya.shape
    D = w_branch.shape[3]
    tm, tn = min(tm, M), min(tn, D)
    nj = D // tn
    y_spec = pl.BlockSpec((tm, K), lambda i, j: (i, 0))
    w_specs = [pl.BlockSpec((None, None, K, tn), functools.partial(lambda i, j, b: (l, b, 0, j), b=b))
               for b in range(N_BRANCH)]
    g_specs = [pl.BlockSpec((tm, tn), functools.partial(lambda i, j, b: (i, b * nj + j), b=b)) for b in range(N_BRANCH)]
    return pl.pallas_call(
        _merge_body,
        grid=(M // tm, nj),
        in_specs=[y_spec, y_spec, y_spec] + w_specs + g_specs,
        out_specs=pl.BlockSpec((tm, tn), lambda i, j: (i, j)),
        out_shape=jax.ShapeDtypeStruct((M, D), BF16),
        compiler_params=_params("parallel", "parallel"),
        name="merge",
    )(ya, yb, yc, w_branch, w_branch, w_branch, gates, gates, gates)


def _softmax_fold(t, tmax, shift, vt, m_prev, l_prev, acc_ref, idx):
    m_new = jnp.maximum(m_prev, tmax + shift)
    p = jnp.exp2(t - (m_new - shift))
    alpha = jnp.exp2(m_prev - m_new)
    l_new = alpha * l_prev + jnp.sum(p, axis=0, keepdims=True)
    acc_ref[idx] = alpha * acc_ref[idx] + jnp.dot(vt, p.astype(BF16), preferred_element_type=F32)
    return m_new, l_new


def _pipelined_tiles(n_last, logits_fn, fold_fn, stats):
    def piped(j, carry, slot):
        stats, maxima = carry
        nxt = logits_fn(j + 1, 1 - slot)
        return fold_fn(j, slot, maxima, stats), nxt

    def body(j, carry):
        return lax.cond((j & 1) == 0, lambda c: piped(j, c, 0), lambda c: piped(j, c, 1), carry)

    carry = lax.fori_loop(0, n_last, body, (stats, logits_fn(0, 0)))
    return lax.cond((n_last & 1) == 0,
                    lambda c: fold_fn(n_last, 0, c[1], c[0]),
                    lambda c: fold_fn(n_last, 1, c[1], c[0]), carry)


def _alibi_bias_tiles(bias_ref, tk, tq, slope):
    off = lax.broadcasted_iota(jnp.int32, (tk, tq), 0) - lax.broadcasted_iota(jnp.int32, (tk, tq), 1)
    rel = off.astype(F32) * slope
    bias_ref[0] = rel
    for r in range(tk // tq):
        bias_ref[1 + r] = jnp.where(off <= r * tq, rel, NEG)


def _diff_body(slopes_ref, lam_ref, q_ref, k_ref, vt_ref, g_ref, o_ref, acc_ref, s_ref, bias_ref, *, tq, tk,
               lam_init, G):
    hp = pl.program_id(1)
    qi = pl.program_id(2)
    W = 2 * HEAD_DIM
    scale = HEAD_DIM ** -0.5 * LOG2E
    slopes = [slopes_ref[hp * G + g] * LOG2E for g in range(G)]
    lp = lam_ref[...]
    lam = (jnp.exp(jnp.sum(lp[0:1] * lp[1:2], axis=-1, keepdims=True))
           - jnp.exp(jnp.sum(lp[2:3] * lp[3:4], axis=-1, keepdims=True)) + lam_init)
    q0 = qi * tq
    n_last = q0 // tk
    last_bias = 1 + (q0 - n_last * tk) // tq

    @pl.when(qi == 0)
    def _():
        for g in range(G):
            _alibi_bias_tiles(bias_ref.at[g], tk, tq, slopes[g])

    acc_ref[...] = jnp.zeros(acc_ref.shape, F32)

    def logits(j, slot):
        k0 = pl.multiple_of(j * tk, tk)
        bidx = jnp.where(j == n_last, last_bias, 0)
        maxima = ()
        for g in range(G):
            bias = bias_ref[g, bidx]
            for c in range(2):
                cs = slice(g * W + c * HEAD_DIM, g * W + (c + 1) * HEAD_DIM)
                t = lax.dot_general(k_ref[pl.ds(k0, tk), cs], q_ref[:, cs], NT_DIMS,
                                    preferred_element_type=F32) * scale + bias
                s_ref[slot, 2 * g + c] = t
                maxima += (jnp.max(t, axis=0, keepdims=True),)
        return maxima

    def fold(j, slot, maxima, stats):
        out = ()
        for g in range(G):
            shift = -slopes[g] * (q0 - j * tk).astype(F32)
            vt = vt_ref[g * W:(g + 1) * W, pl.ds(pl.multiple_of(j * tk, tk), tk)]
            for i in (2 * g, 2 * g + 1):
                out += _softmax_fold(s_ref[slot, i], maxima[i], shift, vt, stats[2 * i], stats[2 * i + 1], acc_ref, i)
        return out

    stats = (jnp.full((1, tq), NEG, F32), jnp.zeros((1, tq), F32)) * (2 * G)
    stats = _pipelined_tiles(n_last, logits, fold, stats)

    for g in range(G):
        l0, l1 = stats[4 * g + 1], stats[4 * g + 3]
        ot = acc_ref[2 * g] * (1.0 / l0) - acc_ref[2 * g + 1] * (lam * (1.0 / l1))
        o = ot.T
        ms = jnp.mean(o * o, axis=-1, keepdims=True)
        y = (o * lax.rsqrt(ms + EPS) * g_ref[...]) * (1.0 - lam_init)
        o_ref[:, g * W:(g + 1) * W] = y.astype(o_ref.dtype)


def _diff_attention(zqk, vt, diff_lambda, norm_g, slopes, batch, lam_init, tq=ATTN_TQ, tk=ATTN_TK,
                    G=DIFF_HEADS_PER_STEP):
    T = zqk.shape[0]
    S = T // batch
    H = DIFF_HEADS
    nq = S // tq
    assert S % tk == 0 and tk % tq == 0 and H % G == 0
    W = 2 * HEAD_DIM
    smem = pl.BlockSpec(memory_space=pltpu.SMEM)
    return pl.pallas_call(
        functools.partial(_diff_body, tq=tq, tk=tk, lam_init=lam_init, G=G),
        grid=(batch, H // G, nq),
        in_specs=[smem,
                  pl.BlockSpec((4, HEAD_DIM), lambda b, h, i: (0, 0)),
                  pl.BlockSpec((tq, G * W), lambda b, h, i: (b * nq + i, h)),
                  pl.BlockSpec((S, G * W), lambda b, h, i: (b, H // G + h)),
                  pl.BlockSpec((G * W, S), lambda b, h, i: (h, b)),
                  pl.BlockSpec((1, W), lambda b, h, i: (0, 0))],
        out_specs=pl.BlockSpec((tq, G * W), lambda b, h, i: (b * nq + i, h)),
        out_shape=jax.ShapeDtypeStruct((T, H * W), BF16),
        scratch_shapes=[pltpu.VMEM((2 * G, W, tq), F32), pltpu.VMEM((2, 2 * G, tk, tq), F32),
                        pltpu.VMEM((G, 1 + tk // tq, tk, tq), F32)],
        compiler_params=_params("parallel", "parallel", "arbitrary"),
        name="diff_attention",
    )(slopes, diff_lambda.astype(F32), zqk, zqk, vt, norm_g.reshape(1, W).astype(F32))


def _moba_allowed(q, km, own, nb):
    nbp, nq = km.shape[0], q.shape[0]
    km_hi = km.astype(BF16)
    km_lo = (km - km_hi.astype(F32)).astype(BF16)
    gs = (lax.dot_general(km_hi, q, NT_DIMS, preferred_element_type=F32)
          + lax.dot_general(km_lo, q, NT_DIMS, preferred_element_type=F32))
    blk = lax.broadcasted_iota(jnp.int32, (nbp, nq), 0)
    rank = jnp.zeros((nbp, nq), F32)
    for m in range(nb):
        row = gs[m:m + 1, :]
        tie = jnp.where(blk > m, 1.0, 0.0)
        beats = jnp.where(row > gs, 1.0, jnp.where(row == gs, tie, 0.0))
        rank = rank + jnp.where(own > m, beats, 0.0)
    chosen = jnp.where(rank < float(min(MOBA_TOPK, nb)), 1.0, 0.0)
    return jnp.where(blk < own, chosen, jnp.where(blk == own, 1.0, 0.0))


def _moba_body(slopes_ref, q_ref, k_ref, vt_ref, o_ref, km_ref, ch_ref, acc_ref, s_ref, bias_ref, *, nb, tk, G):
    hp = pl.program_id(1)
    qi = pl.program_id(2)
    BLK = MOBA_BLOCK
    tq = q_ref.shape[0]
    bpt = tk // BLK
    scale = HEAD_DIM ** -0.5 * LOG2E
    slopes = [slopes_ref[hp * G + g] * LOG2E for g in range(G)]
    cols = [slice(g * HEAD_DIM, (g + 1) * HEAD_DIM) for g in range(G)]
    q0 = qi * tq
    n_last = q0 // tk
    last_bias = 1 + (q0 - n_last * tk) // tq
    own = (q0 + lax.broadcasted_iota(jnp.int32, (1, tq), 1)) // BLK

    @pl.when(qi == 0)
    def _():
        km_ref[...] = jnp.zeros(km_ref.shape, F32)
        for g in range(G):
            _alibi_bias_tiles(bias_ref.at[g], tk, tq, slopes[g])
            for n in range(nb):
                kb = k_ref[n * BLK:(n + 1) * BLK, cols[g]].astype(F32)
                km_ref[g, n:n + 1, :] = jnp.mean(kb, axis=0, keepdims=True)

    for g in range(G):
        ch_ref[g] = _moba_allowed(q_ref[:, cols[g]], km_ref[g], own, nb)
    acc_ref[...] = jnp.zeros(acc_ref.shape, F32)

    def logits(j, slot):
        k0 = pl.multiple_of(j * tk, tk)
        bidx = jnp.where(j == n_last, last_bias, 0)
        maxima = ()
        for g in range(G):
            t = lax.dot_general(k_ref[pl.ds(k0, tk), cols[g]], q_ref[:, cols[g]], NT_DIMS,
                                preferred_element_type=F32) * scale + bias_ref[g, bidx]
            parts = []
            for i in range(bpt):
                allowed = ch_ref[g, pl.ds(j * bpt + i, 1), :]
                parts.append(jnp.where(allowed > 0.5, t[i * BLK:(i + 1) * BLK], NEG))
            t = jnp.concatenate(parts, axis=0)
            s_ref[slot, g] = t
            maxima += (jnp.max(t, axis=0, keepdims=True),)
        return maxima

    def fold(j, slot, maxima, stats):
        out = ()
        for g in range(G):
            shift = -slopes[g] * (q0 - j * tk).astype(F32)
            vt = vt_ref[cols[g], pl.ds(pl.multiple_of(j * tk, tk), tk)]
            out += _softmax_fold(s_ref[slot, g], maxima[g], shift, vt, stats[2 * g], stats[2 * g + 1], acc_ref, g)
        return out

    stats = (jnp.full((1, tq), NEG, F32), jnp.zeros((1, tq), F32)) * G
    stats = _pipelined_tiles(n_last, logits, fold, stats)
    for g in range(G):
        o_ref[:, cols[g]] = (acc_ref[g] * (1.0 / stats[2 * g + 1])).T.astype(o_ref.dtype)


def _moba_attention(zqk, vt, slopes, batch, tq=ATTN_TQ, tk=ATTN_TK, G=MOBA_HEADS_PER_STEP):
    T = zqk.shape[0]
    S = T // batch
    H = MOBA_HEADS
    BLK = MOBA_BLOCK
    assert S % tk == 0 and tk % tq == 0 and tq % BLK == 0 and H % G == 0
    nb = S // BLK
    nq = S // tq
    nbp = -(-nb // 16) * 16
    W = G * HEAD_DIM
    smem = pl.BlockSpec(memory_space=pltpu.SMEM)
    return pl.pallas_call(
        functools.partial(_moba_body, nb=nb, tk=tk, G=G),
        grid=(batch, H // G, nq),
        in_specs=[smem,
                  pl.BlockSpec((tq, W), lambda b, h, i: (b * nq + i, h)),
                  pl.BlockSpec((S, W), lambda b, h, i: (b, H // G + h)),
                  pl.BlockSpec((W, S), lambda b, h, i: (h, b))],
        out_specs=pl.BlockSpec((tq, W), lambda b, h, i: (b * nq + i, h)),
        out_shape=jax.ShapeDtypeStruct((T, H * HEAD_DIM), BF16),
        scratch_shapes=[pltpu.VMEM((G, nbp, HEAD_DIM), F32), pltpu.VMEM((G, nbp, tq), F32),
                        pltpu.VMEM((G, HEAD_DIM, tq), F32), pltpu.VMEM((2, G, tk, tq), F32),
                        pltpu.VMEM((G, 1 + tk // tq, tk, tq), F32)],
        compiler_params=_params("parallel", "parallel", "arbitrary"),
        name="moba_attention",
    )(slopes, zqk, zqk, vt)


def _log_sigmoid(x):
    return jnp.minimum(x, 0.0) - jnp.log(1.0 + jnp.exp(-jnp.abs(x)))


def _split3(x):
    x1 = x.astype(BF16)
    r1 = x - x1.astype(F32)
    x2 = r1.astype(BF16)
    x3 = (r1 - x2.astype(F32)).astype(BF16)
    return x1, x2, x3


def _mlstm_body(qk_ref, v_ref, og_ref, ifc_ref, ifr_ref, cw_ref, cb_ref, gbc_ref, gbr_ref, ng_ref, out_ref,
                xext_ref, ct_ref, n_ref, m_ref, *, L):
    H, DK, DV = ML_HEADS, HEAD_DIM, ML_V
    PAD = SUBLANES
    c = pl.program_id(1)

    @pl.when(c == 0)
    def _():
        xext_ref[0:PAD, :] = jnp.zeros((PAD, xext_ref.shape[1]), F32)
        ct_ref[...] = jnp.zeros(ct_ref.shape, F32)
        n_ref[...] = jnp.zeros(n_ref.shape, F32)
        m_ref[...] = jnp.zeros(m_ref.shape, F32)

    xext_ref[PAD:PAD + L, :] = qk_ref[...].astype(F32)

    def conv_silu(col0):
        cs = slice(col0, col0 + DK)
        y = cb_ref[:, cs] + xext_ref[PAD:PAD + L, cs] * cw_ref[CONV_W - 1:CONV_W, cs]
        for j in range(CONV_W - 1):
            off = PAD - (CONV_W - 1) + j
            y = y + xext_ref[off:off + L, cs] * cw_ref[j:j + 1, cs]
        return y * jax.nn.sigmoid(y)

    rt = lax.broadcasted_iota(jnp.int32, (L, L), 0)
    cl = lax.broadcasted_iota(jnp.int32, (L, L), 1)
    tril = rt >= cl
    ones_tril = jnp.where(tril, 1.0, 0.0).astype(BF16)
    strict = rt > cl

    for h in range(H):
        qh = conv_silu(h * DK)
        kh = conv_silu(H * DK + h * DK) * (DK ** -0.5)
        qb = qh.astype(BF16)
        kb = kh.astype(BF16)
        vh = v_ref[:, h * DV:(h + 1) * DV]
        i_col = ifc_ref[:, h:h + 1] + gbc_ref[:, h:h + 1]
        lf_col = _log_sigmoid(ifc_ref[:, H + h:H + h + 1] + gbc_ref[:, H + h:H + h + 1])
        i_row = ifr_ref[h:h + 1, :] + gbr_ref[h:h + 1, :]
        b1, b2, b3 = _split3(jnp.where(strict, lf_col, 0.0))
        dp = (jnp.dot(ones_tril, b1, preferred_element_type=F32)
              + jnp.dot(ones_tril, b2, preferred_element_type=F32)
              + jnp.dot(ones_tril, b3, preferred_element_type=F32))
        g_col = dp[:, 0:1] + lf_col[0:1, :]
        g_last = g_col[L - 1:L, :]
        m_prev = m_ref[h][:, 0:1]
        d = jnp.where(tril, dp + i_row, NEG)
        inter = g_col + m_prev
        m_t = jnp.maximum(inter, jnp.max(d, axis=-1, keepdims=True))
        w = jnp.exp(d - m_t)
        a = jnp.exp(inter - m_t)
        sw = lax.dot_general(qb, kb, NT_DIMS, preferred_element_type=F32) * w
        num = (a * jnp.dot(qb, ct_ref[h].astype(BF16), preferred_element_type=F32)
               + jnp.dot(sw.astype(BF16), vh, preferred_element_type=F32))
        den = a * jnp.sum(qh * n_ref[h], axis=-1, keepdims=True) + jnp.sum(sw, axis=-1, keepdims=True)
        hh = num / jnp.maximum(jnp.abs(den), jnp.exp(-m_t))
        a_last = g_last - g_col + i_col
        m_new = jnp.maximum(g_last + m_prev, jnp.max(a_last, axis=0, keepdims=True))
        kw = kh * jnp.exp(a_last - m_new)
        decay = jnp.exp(g_last + m_prev - m_new)
        ct_ref[h] = decay * ct_ref[h] + lax.dot_general(kw.astype(BF16), vh, TN_DIMS, preferred_element_type=F32)
        n_ref[h] = decay * n_ref[h] + jnp.sum(kw, axis=0, keepdims=True)
        m_ref[h] = jnp.broadcast_to(m_new, (1, LANES))
        mu = jnp.mean(hh, axis=-1, keepdims=True)
        xc = hh - mu
        var = jnp.mean(xc * xc, axis=-1, keepdims=True)
        vs = slice(h * DV, (h + 1) * DV)
        yn = xc * lax.rsqrt(var + EPS) * ng_ref[:, vs]
        out_ref[:, vs] = (yn * jax.nn.sigmoid(og_ref[:, vs].astype(F32))).astype(out_ref.dtype)

    xext_ref[0:PAD, :] = xext_ref[L:L + PAD, :]


def _mlstm(zm, zif, conv_w, conv_b, gate_b, norm_g, batch, L=256):
    T = zm.shape[0]
    S = T // batch
    H = ML_HEADS
    L = min(L, S)
    nc = S // L
    W = 2 * H * HEAD_DIM
    zif_row = zif.reshape(batch, S, 2 * H).transpose(0, 2, 1)
    gb = gate_b.astype(F32).reshape(2 * H)
    row = lambda b, c: (b * nc + c, 0)
    return pl.pallas_call(
        functools.partial(_mlstm_body, L=L),
        grid=(batch, nc),
        in_specs=[pl.BlockSpec((L, W), row),
                  pl.BlockSpec((L, W), lambda b, c: (b * nc + c, 1)),
                  pl.BlockSpec((L, W), lambda b, c: (b * nc + c, 2)),
                  pl.BlockSpec((L, 2 * H), row),
                  pl.BlockSpec((None, 2 * H, L), lambda b, c: (b, 0, c)),
                  pl.BlockSpec((CONV_W, W), lambda b, c: (0, 0)),
                  pl.BlockSpec((1, W), lambda b, c: (0, 0)),
                  pl.BlockSpec((1, 2 * H), lambda b, c: (0, 0)),
                  pl.BlockSpec((2 * H, 1), lambda b, c: (0, 0)),
                  pl.BlockSpec((1, W), lambda b, c: (0, 0))],
        out_specs=pl.BlockSpec((L, W), row),
        out_shape=jax.ShapeDtypeStruct((T, W), BF16),
        scratch_shapes=[pltpu.VMEM((L + 2 * SUBLANES, W), F32),
                        pltpu.VMEM((H, HEAD_DIM, ML_V), F32),
                        pltpu.VMEM((H, 1, HEAD_DIM), F32),
                        pltpu.VMEM((H, 1, LANES), F32)],
        compiler_params=_params("parallel", "arbitrary"),
        name="mlstm",
    )(zm, zm, zm, zif, zif_row, conv_w.astype(F32), conv_b.reshape(1, W).astype(F32),
      gb.reshape(1, 2 * H), gb.reshape(2 * H, 1), norm_g.reshape(1, W).astype(F32))


def _alibi_slopes(n):
    return jnp.asarray(2.0 ** (-8.0 * jnp.arange(1, n + 1, dtype=F32) / n), dtype=F32)


def _split_w_in(w_in):
    c1 = 6 * DIFF_HEADS * DIFF_V
    c2 = c1 + 2 * ML_HEADS
    w_if = jnp.pad(w_in[:, :, c1:c2], ((0, 0), (0, 0), (0, LANES - 2 * ML_HEADS))).astype(BF16)
    return w_in.astype(BF16), w_if, w_in[:, :, c2:].astype(BF16)


def _mixer(xf, batch, l, norm1_g, w_in, diff_lambda, diff_norm_g, ml_conv_w, ml_conv_b, ml_gate_b, ml_norm_g,
           w_branch, w_out):
    BW = DIFF_HEADS * DIFF_V
    w_lo, w_if, w_hi = w_in
    h = _rmsnorm(xf, norm1_g, BF16)
    zd = _matmul(h, w_lo, l, BF16, PROJ_TM, PROJ_TN, cols=(0, 2 * BW), name="proj_diff_qk")
    vd = _matmul_t(h, w_lo, l, BF16, PROJ_TM, PROJ_TN, cols=(2 * BW, 3 * BW), name="proj_diff_v")
    zm = _matmul(h, w_lo, l, BF16, PROJ_TM, PROJ_TN, cols=(3 * BW, 6 * BW), name="proj_mlstm")
    zif = _matmul(h, w_if, l, F32, 1024, LANES, name="proj_if")[:, :2 * ML_HEADS]
    zb = _matmul(h, w_hi, l, BF16, PROJ_TM, PROJ_TN, cols=(0, 2 * BW), name="proj_moba_qk")
    vb = _matmul_t(h, w_hi, l, BF16, PROJ_TM, PROJ_TN, cols=(2 * BW, 3 * BW), name="proj_moba_v")
    zg = _matmul(h, w_hi, l, BF16, PROJ_TM, PROJ_TN, cols=(3 * BW, w_hi.shape[2]), name="proj_gates")

    lam_init = 0.8 - 0.6 * math.exp(-0.3 * l)
    ya = _diff_attention(zd, vd, diff_lambda, diff_norm_g, _alibi_slopes(DIFF_HEADS), batch, lam_init)
    yb = _mlstm(zm, zif, ml_conv_w, ml_conv_b, ml_gate_b, ml_norm_g, batch)
    yc = _moba_attention(zb, vb, _alibi_slopes(MOBA_HEADS), batch)
    merged = _merge(ya, yb, yc, w_branch, l, zg)
    return _matmul(merged, w_out, l, F32, 1024, 512, residual=xf, name="proj_out")


def _ffn(xf, l, norm2_g, w_gate_up, w_down):
    h = _rmsnorm(xf, norm2_g, BF16)
    act = _swiglu_up(h, w_gate_up, l)
    return _matmul(act, w_down, l, F32, 512, 256, residual=xf, name="ffn_down")


def kernel(x, norm1_g, w_in, diff_lambda, diff_norm_g, ml_conv_w, ml_conv_b, ml_gate_b, ml_norm_g, w_branch, w_out,
           norm2_g, w_gate_up, w_down, final_g):
    B, S, D = x.shape
    xf = x.reshape(B * S, D)
    w_in_b = _split_w_in(w_in)
    w_branch_b, w_out_b = w_branch.astype(BF16), w_out.astype(BF16)
    w_gate_up_b, w_down_b = w_gate_up.astype(BF16), w_down.astype(BF16)
    for l in range(w_in.shape[0]):
        xf = _mixer(xf, B, l, norm1_g[l], w_in_b, diff_lambda[l], diff_norm_g[l], ml_conv_w[l], ml_conv_b[l],
                    ml_gate_b[l], ml_norm_g[l], w_branch_b, w_out_b)
        xf = _ffn(xf, l, norm2_g[l], w_gate_up_b, w_down_b)
    return _rmsnorm(xf, final_g, F32).reshape(B, S, D)
```

```python
import functools
import math

import jax
import jax.numpy as jnp
from jax import lax
from jax.experimental import pallas as pl
from jax.experimental.pallas import tpu as pltpu

F32 = jnp.float32
BF16 = jnp.bfloat16

HEAD_DIM = 128
DIFF_HEADS = 8
DIFF_V = 2 * HEAD_DIM
ML_HEADS = 8
ML_V = 2 * HEAD_DIM
CONV_W = 4
MOBA_HEADS = 16
MOBA_BLOCK = 256
MOBA_TOPK = 3
N_BRANCH = 3
EPS = 1e-6
NEG = -1e30
LOG2E = math.log2(math.e)

LANES = 128
SUBLANES = 8
VMEM_LIMIT = 48 * 1024 * 1024

PROJ_TM = 1024
PROJ_TN = 1024
ATTN_TQ = 512
ATTN_TK = 512
MOBA_HEADS_PER_STEP = 4
DIFF_HEADS_PER_STEP = 2

NT_DIMS = (((1,), (1,)), ((), ()))
TN_DIMS = (((0,), (0,)), ((), ()))


def _params(*sem):
    return pltpu.CompilerParams(dimension_semantics=sem, vmem_limit_bytes=VMEM_LIMIT)


def _rmsnorm_body(x_ref, g_ref, o_ref):
    x = x_ref[...]
    ms = jnp.mean(x * x, axis=-1, keepdims=True)
    o_ref[...] = (x * lax.rsqrt(ms + EPS) * g_ref[...]).astype(o_ref.dtype)


def _rmsnorm(x, g, out_dtype, tr=256):
    T, D = x.shape
    tr = min(tr, T)
    return pl.pallas_call(
        _rmsnorm_body,
        grid=(T // tr,),
        in_specs=[pl.BlockSpec((tr, D), lambda i: (i, 0)), pl.BlockSpec((1, D), lambda i: (0, 0))],
        out_specs=pl.BlockSpec((tr, D), lambda i: (i, 0)),
        out_shape=jax.ShapeDtypeStruct((T, D), out_dtype),
        compiler_params=_params("parallel"),
        name="rmsnorm",
    )(x, g.reshape(1, D).astype(F32))


def _mm_body(a_ref, b_ref, o_ref):
    o_ref[...] = jnp.dot(a_ref[...], b_ref[...], preferred_element_type=F32).astype(o_ref.dtype)


def _mm_res_body(a_ref, b_ref, r_ref, o_ref):
    o_ref[...] = (r_ref[...] + jnp.dot(a_ref[...], b_ref[...], preferred_element_type=F32)).astype(o_ref.dtype)


def _weight_spec(w, l, cols, tn):
    c0, c1 = cols if cols is not None else (0, w.shape[2])
    assert c0 % tn == 0 and (c1 - c0) % tn == 0
    return pl.BlockSpec((None, w.shape[1], tn), lambda i, j: (l, 0, c0 // tn + j)), c1 - c0


def _matmul(a, w, l, out_dtype, tm, tn, residual=None, cols=None, name="matmul"):
    M, K = a.shape
    tm = min(tm, M)
    w_spec, N = _weight_spec(w, l, cols, tn)
    assert M % tm == 0
    in_specs = [pl.BlockSpec((tm, K), lambda i, j: (i, 0)), w_spec]
    args = [a, w]
    body = _mm_body
    if residual is not None:
        in_specs.append(pl.BlockSpec((tm, tn), lambda i, j: (i, j)))
        args.append(residual)
        body = _mm_res_body
    return pl.pallas_call(
        body,
        grid=(M // tm, N // tn),
        in_specs=in_specs,
        out_specs=pl.BlockSpec((tm, tn), lambda i, j: (i, j)),
        out_shape=jax.ShapeDtypeStruct((M, N), out_dtype),
        compiler_params=_params("parallel", "parallel"),
        name=name,
    )(*args)


def _mm_t_body(a_ref, b_ref, o_ref):
    o_ref[...] = jnp.dot(a_ref[...], b_ref[...], preferred_element_type=F32).T.astype(o_ref.dtype)


def _matmul_t(a, w, l, out_dtype, tm, tn, cols=None, name="matmul_t"):
    M, K = a.shape
    tm = min(tm, M)
    w_spec, N = _weight_spec(w, l, cols, tn)
    assert M % tm == 0
    return pl.pallas_call(
        _mm_t_body,
        grid=(M // tm, N // tn),
        in_specs=[pl.BlockSpec((tm, K), lambda i, j: (i, 0)), w_spec],
        out_specs=pl.BlockSpec((tn, tm), lambda i, j: (j, i)),
        out_shape=jax.ShapeDtypeStruct((N, M), out_dtype),
        compiler_params=_params("parallel", "parallel"),
        name=name,
    )(a, w)


def _glu_body(a_ref, wg_ref, wu_ref, o_ref):
    a = a_ref[...]
    g = jnp.dot(a, wg_ref[...].astype(a.dtype), preferred_element_type=F32)
    u = jnp.dot(a, wu_ref[...].astype(a.dtype), preferred_element_type=F32)
    o_ref[...] = (g * jax.nn.sigmoid(g) * u).astype(o_ref.dtype)


def _swiglu_up(a, w_gu, l, tm=1024, tn=256):
    M, K = a.shape
    F = w_gu.shape[2] // 2
    tm = min(tm, M)
    nj = F // tn
    assert F % tn == 0 and M % tm == 0
    return pl.pallas_call(
        _glu_body,
        grid=(M // tm, nj),
        in_specs=[pl.BlockSpec((tm, K), lambda i, j: (i, 0)),
                  pl.BlockSpec((None, K, tn), lambda i, j: (l, 0, j)),
                  pl.BlockSpec((None, K, tn), lambda i, j: (l, 0, nj + j))],
        out_specs=pl.BlockSpec((tm, tn), lambda i, j: (i, j)),
        out_shape=jax.ShapeDtypeStruct((M, F), BF16),
        compiler_params=_params("parallel", "parallel"),
        name="swiglu_up",
    )(a, w_gu, w_gu)


def _merge_body(ya_ref, yb_ref, yc_ref, w0_ref, w1_ref, w2_ref, g0_ref, g1_ref, g2_ref, o_ref):
    acc = jax.nn.sigmoid(g0_ref[...].astype(F32)) * jnp.dot(ya_ref[...], w0_ref[...], preferred_element_type=F32)
    acc += jax.nn.sigmoid(g1_ref[...].astype(F32)) * jnp.dot(yb_ref[...], w1_ref[...], preferred_element_type=F32)
    acc += jax.nn.sigmoid(g2_ref[...].astype(F32)) * jnp.dot(yc_ref[...], w2_ref[...], preferred_element_type=F32)
    o_ref[...] = acc.astype(o_ref.dtype)


def _merge(ya, yb, yc, w_branch, l, gates, tm=512, tn=512):
    M, K = ya.shape
    D = w_branch.shape[3]
    tm, tn = min(tm, M), min(tn, D)
    nj = D // tn
    y_spec = pl.BlockSpec((tm, K), lambda i, j: (i, 0))
    w_specs = [pl.BlockSpec((None, None, K, tn), functools.partial(lambda i, j, b: (l, b, 0, j), b=b))
               for b in range(N_BRANCH)]
    g_specs = [pl.BlockSpec((tm, tn), functools.partial(lambda i, j, b: (i, b * nj + j), b=b)) for b in range(N_BRANCH)]
    return pl.pallas_call(
        _merge_body,
        grid=(M // tm, nj),
        in_specs=[y_spec, y_spec, y_spec] + w_specs + g_specs,
        out_specs=pl.BlockSpec((tm, tn), lambda i, j: (i, j)),
        out_shape=jax.ShapeDtypeStruct((M, D), BF16),
        compiler_params=_params("parallel", "parallel"),
        name="merge",
    )(ya, yb, yc, w_branch, w_branch, w_branch, gates, gates, gates)


def _softmax_fold(t, tmax, shift, vt, m_prev, l_prev, acc_ref, idx):
    m_new = jnp.maximum(m_prev, tmax + shift)
    p = jnp.exp2(t - (m_new - shift))
    alpha = jnp.exp2(m_prev - m_new)
    l_new = alpha * l_prev + jnp.sum(p, axis=0, keepdims=True)
    acc_ref[idx] = alpha * acc_ref[idx] + jnp.dot(vt, p.astype(BF16), preferred_element_type=F32)
    return m_new, l_new


def _pipelined_tiles(n_last, logits_fn, fold_fn, stats):
    def piped(j, carry, slot):
        stats, maxima = carry
        nxt = logits_fn(j + 1, 1 - slot)
        return fold_fn(j, slot, maxima, stats), nxt

    def body(j, carry):
        return lax.cond((j & 1) == 0, lambda c: piped(j, c, 0), lambda c: piped(j, c, 1), carry)

    carry = lax.fori_loop(0, n_last, body, (stats, logits_fn(0, 0)))
    return lax.cond((n_last & 1) == 0,
                    lambda c: fold_fn(n_last, 0, c[1], c[0]),
                    lambda c: fold_fn(n_last, 1, c[1], c[0]), carry)


def _alibi_bias_tiles(bias_ref, tk, tq, slope):
    off = lax.broadcasted_iota(jnp.int32, (tk, tq), 0) - lax.broadcasted_iota(jnp.int32, (tk, tq), 1)
    rel = off.astype(F32) * slope
    bias_ref[0] = rel
    for r in range(tk // tq):
        bias_ref[1 + r] = jnp.where(off <= r * tq, rel, NEG)


def _diff_body(slopes_ref, lam_ref, q_ref, k_ref, vt_ref, g_ref, o_ref, acc_ref, s_ref, bias_ref, *, tq, tk,
               lam_init, G):
    hp = pl.program_id(1)
    qi = pl.program_id(2)
    W = 2 * HEAD_DIM
    scale = HEAD_DIM ** -0.5 * LOG2E
    slopes = [slopes_ref[hp * G + g] * LOG2E for g in range(G)]
    lp = lam_ref[...]
    lam = (jnp.exp(jnp.sum(lp[0:1] * lp[1:2], axis=-1, keepdims=True))
           - jnp.exp(jnp.sum(lp[2:3] * lp[3:4], axis=-1, keepdims=True)) + lam_init)
    q0 = qi * tq
    n_last = q0 // tk
    last_bias = 1 + (q0 - n_last * tk) // tq

    @pl.when(qi == 0)
    def _():
        for g in range(G):
            _alibi_bias_tiles(bias_ref.at[g], tk, tq, slopes[g])

    acc_ref[...] = jnp.zeros(acc_ref.shape, F32)

    def logits(j, slot):
        k0 = pl.multiple_of(j * tk, tk)
        bidx = jnp.where(j == n_last, last_bias, 0)
        maxima = ()
        for g in range(G):
            bias = bias_ref[g, bidx]
            for c in range(2):
                cs = slice(g * W + c * HEAD_DIM, g * W + (c + 1) * HEAD_DIM)
                t = lax.dot_general(k_ref[pl.ds(k0, tk), cs], q_ref[:, cs], NT_DIMS,
                                    preferred_element_type=F32) * scale + bias
                s_ref[slot, 2 * g + c] = t
                maxima += (jnp.max(t, axis=0, keepdims=True),)
        return maxima

    def fold(j, slot, maxima, stats):
        out = ()
        for g in range(G):
            shift = -slopes[g] * (q0 - j * tk).astype(F32)
            vt = vt_ref[g * W:(g + 1) * W, pl.ds(pl.multiple_of(j * tk, tk), tk)]
            for i in (2 * g, 2 * g + 1):
                out += _softmax_fold(s_ref[slot, i], maxima[i], shift, vt, stats[2 * i], stats[2 * i + 1], acc_ref, i)
        return out

    stats = (jnp.full((1, tq), NEG, F32), jnp.zeros((1, tq), F32)) * (2 * G)
    stats = _pipelined_tiles(n_last, logits, fold, stats)

    for g in range(G):
        l0, l1 = stats[4 * g + 1], stats[4 * g + 3]
        ot = acc_ref[2 * g] * (1.0 / l0) - acc_ref[2 * g + 1] * (lam * (1.0 / l1))
        o = ot.T
        ms = jnp.mean(o * o, axis=-1, keepdims=True)
        y = (o * lax.rsqrt(ms + EPS) * g_ref[...]) * (1.0 - lam_init)
        o_ref[:, g * W:(g + 1) * W] = y.astype(o_ref.dtype)


def _diff_attention(zqk, vt, diff_lambda, norm_g, slopes, batch, lam_init, tq=ATTN_TQ, tk=ATTN_TK,
                    G=DIFF_HEADS_PER_STEP):
    T = zqk.shape[0]
    S = T // batch
    H = DIFF_HEADS
    nq = S // tq
    assert S % tk == 0 and tk % tq == 0 and H % G == 0
    W = 2 * HEAD_DIM
    smem = pl.BlockSpec(memory_space=pltpu.SMEM)
    return pl.pallas_call(
        functools.partial(_diff_body, tq=tq, tk=tk, lam_init=lam_init, G=G),
        grid=(batch, H // G, nq),
        in_specs=[smem,
                  pl.BlockSpec((4, HEAD_DIM), lambda b, h, i: (0, 0)),
                  pl.BlockSpec((tq, G * W), lambda b, h, i: (b * nq + i, h)),
                  pl.BlockSpec((S, G * W), lambda b, h, i: (b, H // G + h)),
                  pl.BlockSpec((G * W, S), lambda b, h, i: (h, b)),
                  pl.BlockSpec((1, W), lambda b, h, i: (0, 0))],
        out_specs=pl.BlockSpec((tq, G * W), lambda b, h, i: (b * nq + i, h)),
        out_shape=jax.ShapeDtypeStruct((T, H * W), BF16),
        scratch_shapes=[pltpu.VMEM((2 * G, W, tq), F32), pltpu.VMEM((2, 2 * G, tk, tq), F32),
                        pltpu.VMEM((G, 1 + tk // tq, tk, tq), F32)],
        compiler_params=_params("parallel", "parallel", "arbitrary"),
        name="diff_attention",
    )(slopes, diff_lambda.astype(F32), zqk, zqk, vt, norm_g.reshape(1, W).astype(F32))


def _moba_allowed(q, km, own, nb):
    nbp, nq = km.shape[0], q.shape[0]
    km_hi = km.astype(BF16)
    km_lo = (km - km_hi.astype(F32)).astype(BF16)
    gs = (lax.dot_general(km_hi, q, NT_DIMS, preferred_element_type=F32)
          + lax.dot_general(km_lo, q, NT_DIMS, preferred_element_type=F32))
    blk = lax.broadcasted_iota(jnp.int32, (nbp, nq), 0)
    rank = jnp.zeros((nbp, nq), F32)
    for m in range(nb):
        row = gs[m:m + 1, :]
        tie = jnp.where(blk > m, 1.0, 0.0)
        beats = jnp.where(row > gs, 1.0, jnp.where(row == gs, tie, 0.0))
        rank = rank + jnp.where(own > m, beats, 0.0)
    chosen = jnp.where(rank < float(min(MOBA_TOPK, nb)), 1.0, 0.0)
    return jnp.where(blk < own, chosen, jnp.where(blk == own, 1.0, 0.0))


def _moba_body(slopes_ref, q_ref, k_ref, vt_ref, o_ref, km_ref, ch_ref, acc_ref, s_ref, bias_ref, *, nb, tk, G):
    hp = pl.program_id(1)
    qi = pl.program_id(2)
    BLK = MOBA_BLOCK
    tq = q_ref.shape[0]
    bpt = tk // BLK
    scale = HEAD_DIM ** -0.5 * LOG2E
    slopes = [slopes_ref[hp * G + g] * LOG2E for g in range(G)]
    cols = [slice(g * HEAD_DIM, (g + 1) * HEAD_DIM) for g in range(G)]
    q0 = qi * tq
    n_last = q0 // tk
    last_bias = 1 + (q0 - n_last * tk) // tq
    own = (q0 + lax.broadcasted_iota(jnp.int32, (1, tq), 1)) // BLK

    @pl.when(qi == 0)
    def _():
        km_ref[...] = jnp.zeros(km_ref.shape, F32)
        for g in range(G):
            _alibi_bias_tiles(bias_ref.at[g], tk, tq, slopes[g])
            for n in range(nb):
                kb = k_ref[n * BLK:(n + 1) * BLK, cols[g]].astype(F32)
                km_ref[g, n:n + 1, :] = jnp.mean(kb, axis=0, keepdims=True)

    for g in range(G):
        ch_ref[g] = _moba_allowed(q_ref[:, cols[g]], km_ref[g], own, nb)
    acc_ref[...] = jnp.zeros(acc_ref.shape, F32)

    def logits(j, slot):
        k0 = pl.multiple_of(j * tk, tk)
        bidx = jnp.where(j == n_last, last_bias, 0)
        maxima = ()
        for g in range(G):
            t = lax.dot_general(k_ref[pl.ds(k0, tk), cols[g]], q_ref[:, cols[g]], NT_DIMS,
                                preferred_element_type=F32) * scale + bias_ref[g, bidx]
            parts = []
            for i in range(bpt):
                allowed = ch_ref[g, pl.ds(j * bpt + i, 1), :]
                parts.append(jnp.where(allowed > 0.5, t[i * BLK:(i + 1) * BLK], NEG))
            t = jnp.concatenate(parts, axis=0)
            s_ref[slot, g] = t
            maxima += (jnp.max(t, axis=0, keepdims=True),)
        return maxima

    def fold(j, slot, maxima, stats):
        out = ()
        for g in range(G):
            shift = -slopes[g] * (q0 - j * tk).astype(F32)
            vt = vt_ref[cols[g], pl.ds(pl.multiple_of(j * tk, tk), tk)]
            out += _softmax_fold(s_ref[slot, g], maxima[g], shift, vt, stats[2 * g], stats[2 * g + 1], acc_ref, g)
        return out

    stats = (jnp.full((1, tq), NEG, F32), jnp.zeros((1, tq), F32)) * G
    stats = _pipelined_tiles(n_last, logits, fold, stats)
    for g in range(G):
        o_ref[:, cols[g]] = (acc_ref[g] * (1.0 / stats[2 * g + 1])).T.astype(o_ref.dtype)


def _moba_attention(zqk, vt, slopes, batch, tq=ATTN_TQ, tk=ATTN_TK, G=MOBA_HEADS_PER_STEP):
    T = zqk.shape[0]
    S = T // batch
    H = MOBA_HEADS
    BLK = MOBA_BLOCK
    assert S % tk == 0 and tk % tq == 0 and tq % BLK == 0 and H % G == 0
    nb = S // BLK
    nq = S // tq
    nbp = -(-nb // 16) * 16
    W = G * HEAD_DIM
    smem = pl.BlockSpec(memory_space=pltpu.SMEM)
    return pl.pallas_call(
        functools.partial(_moba_body, nb=nb, tk=tk, G=G),
        grid=(batch, H // G, nq),
        in_specs=[smem,
                  pl.BlockSpec((tq, W), lambda b, h, i: (b * nq + i, h)),
                  pl.BlockSpec((S, W), lambda b, h, i: (b, H // G + h)),
                  pl.BlockSpec((W, S), lambda b, h, i: (h, b))],
        out_specs=pl.BlockSpec((tq, W), lambda b, h, i: (b * nq + i, h)),
        out_shape=jax.ShapeDtypeStruct((T, H * HEAD_DIM), BF16),
        scratch_shapes=[pltpu.VMEM((G, nbp, HEAD_DIM), F32), pltpu.VMEM((G, nbp, tq), F32),
                        pltpu.VMEM((G, HEAD_DIM, tq), F32), pltpu.VMEM((2, G, tk, tq), F32),
                        pltpu.VMEM((G, 1 + tk // tq, tk, tq), F32)],
        compiler_params=_params("parallel", "parallel", "arbitrary"),
        name="moba_attention",
    )(slopes, zqk, zqk, vt)


def _log_sigmoid(x):
    return jnp.minimum(x, 0.0) - jnp.log(1.0 + jnp.exp(-jnp.abs(x)))


def _split3(x):
    x1 = x.astype(BF16)
    r1 = x - x1.astype(F32)
    x2 = r1.astype(BF16)
    x3 = (r1 - x2.astype(F32)).astype(BF16)
    return x1, x2, x3


def _mlstm_body(qk_ref, v_ref, og_ref, ifc_ref, ifr_ref, cw_ref, cb_ref, gbc_ref, gbr_ref, ng_ref, out_ref,
                xext_ref, ct_ref, n_ref, m_ref, *, L):
    H, DK, DV = ML_HEADS, HEAD_DIM, ML_V
    PAD = SUBLANES
    c = pl.program_id(1)

    @pl.when(c == 0)
    def _():
        xext_ref[0:PAD, :] = jnp.zeros((PAD, xext_ref.shape[1]), F32)
        ct_ref[...] = jnp.zeros(ct_ref.shape, F32)
        n_ref[...] = jnp.zeros(n_ref.shape, F32)
        m_ref[...] = jnp.zeros(m_ref.shape, F32)

    xext_ref[PAD:PAD + L, :] = qk_ref[...].astype(F32)

    def conv_silu(col0):
        cs = slice(col0, col0 + DK)
        y = cb_ref[:, cs] + xext_ref[PAD:PAD + L, cs] * cw_ref[CONV_W - 1:CONV_W, cs]
        for j in range(CONV_W - 1):
            off = PAD - (CONV_W - 1) + j
            y = y + xext_ref[off:off + L, cs] * cw_ref[j:j + 1, cs]
        return y * jax.nn.sigmoid(y)

    rt = lax.broadcasted_iota(jnp.int32, (L, L), 0)
    cl = lax.broadcasted_iota(jnp.int32, (L, L), 1)
    tril = rt >= cl
    ones_tril = jnp.where(tril, 1.0, 0.0).astype(BF16)
    strict = rt > cl

    for h in range(H):
        qh = conv_silu(h * DK)
        kh = conv_silu(H * DK + h * DK) * (DK ** -0.5)
        qb = qh.astype(BF16)
        kb = kh.astype(BF16)
        vh = v_ref[:, h * DV:(h + 1) * DV]
        i_col = ifc_ref[:, h:h + 1] + gbc_ref[:, h:h + 1]
        lf_col = _log_sigmoid(ifc_ref[:, H + h:H + h + 1] + gbc_ref[:, H + h:H + h + 1])
        i_row = ifr_ref[h:h + 1, :] + gbr_ref[h:h + 1, :]
        b1, b2, b3 = _split3(jnp.where(strict, lf_col, 0.0))
        dp = (jnp.dot(ones_tril, b1, preferred_element_type=F32)
              + jnp.dot(ones_tril, b2, preferred_element_type=F32)
              + jnp.dot(ones_tril, b3, preferred_element_type=F32))
        g_col = dp[:, 0:1] + lf_col[0:1, :]
        g_last = g_col[L - 1:L, :]
        m_prev = m_ref[h][:, 0:1]
        d = jnp.where(tril, dp + i_row, NEG)
        inter = g_col + m_prev
        m_t = jnp.maximum(inter, jnp.max(d, axis=-1, keepdims=True))
        w = jnp.exp(d - m_t)
        a = jnp.exp(inter - m_t)
        sw = lax.dot_general(qb, kb, NT_DIMS, preferred_element_type=F32) * w
        num = (a * jnp.dot(qb, ct_ref[h].astype(BF16), preferred_element_type=F32)
               + jnp.dot(sw.astype(BF16), vh, preferred_element_type=F32))
        den = a * jnp.sum(qh * n_ref[h], axis=-1, keepdims=True) + jnp.sum(sw, axis=-1, keepdims=True)
        hh = num / jnp.maximum(jnp.abs(den), jnp.exp(-m_t))
        a_last = g_last - g_col + i_col
        m_new = jnp.maximum(g_last + m_prev, jnp.max(a_last, axis=0, keepdims=True))
        kw = kh * jnp.exp(a_last - m_new)
        decay = jnp.exp(g_last + m_prev - m_new)
        ct_ref[h] = decay * ct_ref[h] + lax.dot_general(kw.astype(BF16), vh, TN_DIMS, preferred_element_type=F32)
        n_ref[h] = decay * n_ref[h] + jnp.sum(kw, axis=0, keepdims=True)
        m_ref[h] = jnp.broadcast_to(m_new, (1, LANES))
        mu = jnp.mean(hh, axis=-1, keepdims=True)
        xc = hh - mu
        var = jnp.mean(xc * xc, axis=-1, keepdims=True)
        vs = slice(h * DV, (h + 1) * DV)
        yn = xc * lax.rsqrt(var + EPS) * ng_ref[:, vs]
        out_ref[:, vs] = (yn * jax.nn.sigmoid(og_ref[:, vs].astype(F32))).astype(out_ref.dtype)

    xext_ref[0:PAD, :] = xext_ref[L:L + PAD, :]


def _mlstm(zm, zif, conv_w, conv_b, gate_b, norm_g, batch, L=256):
    T = zm.shape[0]
    S = T // batch
    H = ML_HEADS
    L = min(L, S)
    nc = S // L
    W = 2 * H * HEAD_DIM
    zif_row = zif.reshape(batch, S, 2 * H).transpose(0, 2, 1)
    gb = gate_b.astype(F32).reshape(2 * H)
    row = lambda b, c: (b * nc + c, 0)
    return pl.pallas_call(
        functools.partial(_mlstm_body, L=L),
        grid=(batch, nc),
        in_specs=[pl.BlockSpec((L, W), row),
                  pl.BlockSpec((L, W), lambda b, c: (b * nc + c, 1)),
                  pl.BlockSpec((L, W), lambda b, c: (b * nc + c, 2)),
                  pl.BlockSpec((L, 2 * H), row),
                  pl.BlockSpec((None, 2 * H, L), lambda b, c: (b, 0, c)),
                  pl.BlockSpec((CONV_W, W), lambda b, c: (0, 0)),
                  pl.BlockSpec((1, W), lambda b, c: (0, 0)),
                  pl.BlockSpec((1, 2 * H), lambda b, c: (0, 0)),
                  pl.BlockSpec((2 * H, 1), lambda b, c: (0, 0)),
                  pl.BlockSpec((1, W), lambda b, c: (0, 0))],
        out_specs=pl.BlockSpec((L, W), row),
        out_shape=jax.ShapeDtypeStruct((T, W), BF16),
        scratch_shapes=[pltpu.VMEM((L + 2 * SUBLANES, W), F32),
                        pltpu.VMEM((H, HEAD_DIM, ML_V), F32),
                        pltpu.VMEM((H, 1, HEAD_DIM), F32),
                        pltpu.VMEM((H, 1, LANES), F32)],
        compiler_params=_params("parallel", "arbitrary"),
        name="mlstm",
    )(zm, zm, zm, zif, zif_row, conv_w.astype(F32), conv_b.reshape(1, W).astype(F32),
      gb.reshape(1, 2 * H), gb.reshape(2 * H, 1), norm_g.reshape(1, W).astype(F32))


def _alibi_slopes(n):
    return jnp.asarray(2.0 ** (-8.0 * jnp.arange(1, n + 1, dtype=F32) / n), dtype=F32)


def _split_w_in(w_in):
    c1 = 6 * DIFF_HEADS * DIFF_V
    c2 = c1 + 2 * ML_HEADS
    w_if = jnp.pad(w_in[:, :, c1:c2], ((0, 0), (0, 0), (0, LANES - 2 * ML_HEADS))).astype(BF16)
    return w_in.astype(BF16), w_if, w_in[:, :, c2:].astype(BF16)


def _mixer(xf, batch, l, norm1_g, w_in, diff_lambda, diff_norm_g, ml_conv_w, ml_conv_b, ml_gate_b, ml_norm_g,
           w_branch, w_out):
    BW = DIFF_HEADS * DIFF_V
    w_lo, w_if, w_hi = w_in
    h = _rmsnorm(xf, norm1_g, BF16)
    zd = _matmul(h, w_lo, l, BF16, PROJ_TM, PROJ_TN, cols=(0, 2 * BW), name="proj_diff_qk")
    vd = _matmul_t(h, w_lo, l, BF16, PROJ_TM, PROJ_TN, cols=(2 * BW, 3 * BW), name="proj_diff_v")
    zm = _matmul(h, w_lo, l, BF16, PROJ_TM, PROJ_TN, cols=(3 * BW, 6 * BW), name="proj_mlstm")
    zif = _matmul(h, w_if, l, F32, 1024, LANES, name="proj_if")[:, :2 * ML_HEADS]
    zb = _matmul(h, w_hi, l, BF16, PROJ_TM, PROJ_TN, cols=(0, 2 * BW), name="proj_moba_qk")
    vb = _matmul_t(h, w_hi, l, BF16, PROJ_TM, PROJ_TN, cols=(2 * BW, 3 * BW), name="proj_moba_v")
    zg = _matmul(h, w_hi, l, BF16, PROJ_TM, PROJ_TN, cols=(3 * BW, w_hi.shape[2]), name="proj_gates")

    lam_init = 0.8 - 0.6 * math.exp(-0.3 * l)
    ya = _diff_attention(zd, vd, diff_lambda, diff_norm_g, _alibi_slopes(DIFF_HEADS), batch, lam_init)
    yb = _mlstm(zm, zif, ml_conv_w, ml_conv_b, ml_gate_b, ml_norm_g, batch)
    yc = _moba_attention(zb, vb, _alibi_slopes(MOBA_HEADS), batch)
    merged = _merge(ya, yb, yc, w_branch, l, zg)
    return _matmul(merged, w_out, l, F32, 1024, 512, residual=xf, name="proj_out")


def _ffn(xf, l, norm2_g, w_gate_up, w_down):
    h = _rmsnorm(xf, norm2_g, BF16)
    act = _swiglu_up(h, w_gate_up, l)
    return _matmul(act, w_down, l, F32, 512, 256, residual=xf, name="ffn_down")


def kernel(x, norm1_g, w_in, diff_lambda, diff_norm_g, ml_conv_w, ml_conv_b, ml_gate_b, ml_norm_g, w_branch, w_out,
           norm2_g, w_gate_up, w_down, final_g):
    B, S, D = x.shape
    xf = x.reshape(B * S, D)
    w_in_b = _split_w_in(w_in)
    w_branch_b, w_out_b = w_branch.astype(BF16), w_out.astype(BF16)
    w_gate_up_b, w_down_b = w_gate_up, w_down.astype(BF16)
    for l in range(w_in.shape[0]):
        xf = _mixer(xf, B, l, norm1_g[l], w_in_b, diff_lambda[l], diff_norm_g[l], ml_conv_w[l], ml_conv_b[l],
                    ml_gate_b[l], ml_norm_g[l], w_branch_b, w_out_b)
        xf = _ffn(xf, l, norm2_g[l], w_gate_up_b, w_down_b)
    return _rmsnorm(xf, final_g, F32).reshape(B, S, D)
```
